```python
import jax, jax.numpy as jnp
from jax import lax
import numpy as np

D_MODEL = 2048
BATCH = 1
SEQ = 8192
DEPTH = 1
DEC_BATCH = 2
DEC_SEQ = 8192
PAST_LEN = 128

GRID_W = 64
N_HEADS = D_MODEL // 128
HEAD_DIM = 64
ATT_WIDTH = N_HEADS * HEAD_DIM
N_FGROUPS = 4
FGROUP_DIM = D_MODEL // 8
FOURIER_WIDTH = N_FGROUPS * FGROUP_DIM
WIN_ROWS_MAX = 8
WIN_COLS = 16
D_FF = 4 * D_MODEL
IN_WIDTH = 3 * ATT_WIDTH + FOURIER_WIDTH + 2 * D_MODEL
EPS = 1e-6

kernel_name = 'hybrid_fnet_natten_encoder'


def rmsnorm(x, g):
    xf = x.astype(jnp.float32)
    y = xf * lax.rsqrt(jnp.mean(xf * xf, axis=-1, keepdims=True) + EPS)
    return (y * g.astype(jnp.float32)).astype(x.dtype)


def fourier_mix(f):
    B, S, _ = f.shape
    u = f.astype(jnp.float32).reshape(B, S, N_FGROUPS, FGROUP_DIM)
    y = jnp.fft.fft2(u, axes=(1, 3), norm='ortho').real
    return y.reshape(B, S, FOURIER_WIDTH).astype(f.dtype)


def neighbourhood_attention(q, k, v, rpb):
    B, S, H, Dh = q.shape
    rows = S // GRID_W
    kr = min(WIN_ROWS_MAX, rows)
    qg = q.reshape(B, rows, GRID_W, H, Dh)
    kg = k.reshape(B, rows, GRID_W, H, Dh)
    vg = v.reshape(B, rows, GRID_W, H, Dh)
    cols = jnp.arange(GRID_W)
    cstart = jnp.clip(cols - WIN_COLS // 2, 0, GRID_W - WIN_COLS)
    col_idx = cstart[:, None] + jnp.arange(WIN_COLS)[None, :]
    coff = col_idx - cols[:, None] + (WIN_COLS - 1)
    scale = HEAD_DIM ** -0.5

    def one_row(r):
        rstart = jnp.clip(r - kr // 2, 0, rows - kr)
        kb = lax.dynamic_slice_in_dim(kg, rstart, kr, axis=1)
        vb = lax.dynamic_slice_in_dim(vg, rstart, kr, axis=1)
        qr = lax.dynamic_index_in_dim(qg, r, axis=1, keepdims=False)
        kw = kb[:, :, col_idx]
        vw = vb[:, :, col_idx]
        roff = rstart + jnp.arange(kr) - r + (WIN_ROWS_MAX - 1)
        bias = rpb[:, roff[None, :, None], coff[:, None, :]]
        s = jnp.einsum('bwhd,bawchd->bhwac', qr, kw).astype(jnp.float32) * scale
        s = s + bias.astype(jnp.float32)[None]
        p = jax.nn.softmax(s.reshape(B, H, GRID_W, kr * WIN_COLS), axis=-1)
        p = p.reshape(s.shape).astype(v.dtype)
        return jnp.einsum('bhwac,bawchd->bwhd', p, vw)

    out = lax.map(one_row, jnp.arange(rows))
    return jnp.transpose(out, (1, 0, 2, 3, 4)).reshape(B, S, H * Dh)


def encoder_block(x, g_mix, w_in, q_norm, k_norm, rpb, w_fo, w_ao, w_out, g_mlp, w1, w2):
    B, S, _ = x.shape
    h = rmsnorm(x, g_mix)
    z = h @ w_in
    a0 = ATT_WIDTH
    f0 = 3 * ATT_WIDTH
    g0 = f0 + FOURIER_WIDTH
    q, k, v, f, ga, gb = jnp.split(z, [a0, 2 * a0, f0, g0, g0 + D_MODEL], axis=-1)
    q = rmsnorm(q.reshape(B, S, N_HEADS, HEAD_DIM), q_norm)
    k = rmsnorm(k.reshape(B, S, N_HEADS, HEAD_DIM), k_norm)
    v = v.reshape(B, S, N_HEADS, HEAD_DIM)
    ya = fourier_mix(f) @ w_fo
    yb = neighbourhood_attention(q, k, v, rpb) @ w_ao
    merged = jax.nn.sigmoid(ga) * ya + jax.nn.sigmoid(gb) * yb
    x = x + merged @ w_out
    u = jnp.maximum(rmsnorm(x, g_mlp) @ w1, 0)
    return x + (u * u) @ w2


def setup_inputs(seed: int = 0) -> dict:
    key = jax.random.key(seed)
    ks = jax.random.split(key, 14)
    def w(k, shape, fan_in):
        return jax.random.normal(k, shape, jnp.float32) * fan_in ** -0.5
    def gain(k, shape):
        return 1.0 + 0.01 * jax.random.normal(k, shape, jnp.float32)
    return {
        'x_prompt': jax.random.normal(ks[0], (BATCH, SEQ, D_MODEL), jnp.float32),
        'x_sample': jax.random.normal(ks[1], (DEC_BATCH, DEC_SEQ, D_MODEL), jnp.float32),
        'g_mix': gain(ks[2], (DEPTH, D_MODEL)),
        'w_in': w(ks[3], (DEPTH, D_MODEL, IN_WIDTH), D_MODEL),
        'q_norm': gain(ks[4], (DEPTH, HEAD_DIM)),
        'k_norm': gain(ks[5], (DEPTH, HEAD_DIM)),
        'rpb': 0.1 * jax.random.normal(ks[6], (DEPTH, N_HEADS, 2 * WIN_ROWS_MAX - 1, 2 * WIN_COLS - 1), jnp.float32),
        'w_fo': w(ks[7], (DEPTH, FOURIER_WIDTH, D_MODEL), FOURIER_WIDTH),
        'w_ao': w(ks[8], (DEPTH, ATT_WIDTH, D_MODEL), ATT_WIDTH),
        'w_out': w(ks[9], (DEPTH, D_MODEL, D_MODEL), D_MODEL),
        'g_mlp': gain(ks[10], (DEPTH, D_MODEL)),
        'w1': w(ks[11], (DEPTH, D_MODEL, D_FF), D_MODEL),
        'w2': w(ks[12], (DEPTH, D_FF, D_MODEL), D_FF),
    }


def reference(x_prompt, x_sample, g_mix, w_in, q_norm, k_norm, rpb, w_fo, w_ao, w_out, g_mlp, w1, w2):
    y_prompt = x_prompt
    y_sample = x_sample
    for l in range(DEPTH):
        params = (g_mix[l], w_in[l], q_norm[l], k_norm[l], rpb[l], w_fo[l], w_ao[l], w_out[l], g_mlp[l], w1[l], w2[l])
        y_prompt = encoder_block(y_prompt, *params)
        y_sample = encoder_block(y_sample, *params)
    return (y_prompt, y_sample)
```

```python
import functools

import numpy as np
import jax
import jax.numpy as jnp
from jax import lax
from jax.experimental import pallas as pl
from jax.experimental.pallas import tpu as pltpu

D_MODEL = 2048
GRID_W = 64
N_HEADS = 16
HEAD_DIM = 64
ATT_WIDTH = N_HEADS * HEAD_DIM
N_FGROUPS = 4
FGROUP_DIM = 256
FOURIER_WIDTH = N_FGROUPS * FGROUP_DIM
WIN_ROWS = 8
WIN_COLS = 16
D_FF = 4 * D_MODEL
IN_WIDTH = 3 * ATT_WIDTH + FOURIER_WIDTH + 2 * D_MODEL
EPS = 1e-6

SEQ = 8192
ROWS = SEQ // GRID_W
LANES = 128
N_PAIRS = N_HEADS // 2
NEG_BIAS = -1e30

R1, R2, R3 = 32, 16, 16
CHUNK = 16

VMEM_LIMIT = 56 * 1024 * 1024

bf16 = jnp.bfloat16
f32 = jnp.float32


def _cparams(sem):
    return pltpu.CompilerParams(dimension_semantics=sem, vmem_limit_bytes=VMEM_LIMIT)


def _stage1_matrices():
    mc = np.arange(256 // CHUNK)[:, None, None, None]
    k1 = np.arange(R1)[None, :, None, None]
    ml = np.arange(CHUNK)[None, None, :, None]
    n1 = np.arange(R1)[None, None, None, :]
    theta = 2.0 * np.pi * (((256 * n1 + 16 * mc + ml) * k1) % SEQ) / SEQ
    eye = np.eye(CHUNK)
    scale = 1.0 / np.sqrt(R1)
    out = np.zeros((256 // CHUNK, 2, R1, CHUNK, R1, CHUNK), np.float64)
    for part, fn in enumerate((np.cos, lambda t: -np.sin(t))):
        vals = fn(theta) * scale
        out[:, part] = vals[:, :, :, :, None] * eye[None, None, :, None, :]
    return out.reshape(256 // CHUNK, 2 * R1 * CHUNK, R1 * CHUNK)


def _complex_block(cos, sin):
    return np.block([[cos, sin], [-sin, cos]])


def _stage2_matrix():
    k2 = np.arange(R2)[:, None, None, None]
    n3o = np.arange(R3)[None, :, None, None]
    n2 = np.arange(R2)[None, None, :, None]
    n3 = np.arange(R3)[None, None, None, :]
    phi = 2.0 * np.pi * (((16 * n2 + n3) * k2) % 256) / 256
    delta = (n3o == n3).astype(np.float64)
    scale = 1.0 / np.sqrt(R2)
    cos = (np.cos(phi) * delta * scale).reshape(R2 * R3, R2 * R3)
    sin = (np.sin(phi) * delta * scale).reshape(R2 * R3, R2 * R3)
    return _complex_block(cos, sin)


def _stage3_matrix():
    k3 = np.arange(R3)[:, None, None, None]
    k1o = np.arange(CHUNK)[None, :, None, None]
    k1 = np.arange(CHUNK)[None, None, :, None]
    n3 = np.arange(R3)[None, None, None, :]
    psi = 2.0 * np.pi * ((n3 * k3) % R3) / R3
    delta = (k1o == k1).astype(np.float64)
    scale = 1.0 / np.sqrt(R3)
    cos = (np.cos(psi) * delta * scale).reshape(R3 * CHUNK, CHUNK * R3)
    sin = (np.sin(psi) * delta * scale).reshape(R3 * CHUNK, CHUNK * R3)
    return _complex_block(cos, sin)


def _channel_matrices():
    c = np.arange(FGROUP_DIM)
    ang = 2.0 * np.pi * ((c[:, None] * c[None, :]) % FGROUP_DIM) / FGROUP_DIM
    scale = 1.0 / np.sqrt(FGROUP_DIM)
    return np.cos(ang) * scale, np.sin(ang) * scale


PROJ_TM = 1024
PROJ_TN = 1024


def _rmsnorm_rows(x, g):
    ms = jnp.mean(x * x, axis=-1, keepdims=True)
    return x * lax.rsqrt(ms + EPS) * g


PROJ_CH = 256


def _proj_qk_kernel(x_ref, g_ref, w_ref, gain_ref, o_ref, h_ref):
    j = pl.program_id(1)

    @pl.when(j == 0)
    def _():
        h_ref[...] = _rmsnorm_rows(x_ref[...], g_ref[...]).astype(bf16)

    gain = gain_ref[pl.ds(j, 1), :]
    lane = lax.broadcasted_iota(jnp.int32, (1, LANES), 1)
    lo = lane < HEAD_DIM
    h = h_ref[...]
    for c in range(PROJ_TN // PROJ_CH):
        acc = jnp.dot(h, w_ref[:, c * PROJ_CH:(c + 1) * PROJ_CH], preferred_element_type=f32)
        for p in range(PROJ_CH // LANES):
            a = acc[:, p * LANES:(p + 1) * LANES]
            sq = a * a
            s0 = jnp.sum(jnp.where(lo, sq, 0.0), axis=-1, keepdims=True)
            s1 = jnp.sum(jnp.where(lo, 0.0, sq), axis=-1, keepdims=True)
            r0 = lax.rsqrt(s0 * (1.0 / HEAD_DIM) + EPS)
            r1 = lax.rsqrt(s1 * (1.0 / HEAD_DIM) + EPS)
            col = c * PROJ_CH + p * LANES
            o_ref[:, col:col + LANES] = (a * jnp.where(lo, r0, r1) * gain).astype(bf16)


def _proj_act_kernel(h_ref, w_ref, o_ref, *, sigmoid):
    h = h_ref[...]
    for c in range(PROJ_TN // PROJ_CH):
        cols = slice(c * PROJ_CH, (c + 1) * PROJ_CH)
        acc = jnp.dot(h, w_ref[:, cols], preferred_element_type=f32)
        if sigmoid:
            acc = 1.0 / (1.0 + jnp.exp(-acc))
        o_ref[:, cols] = acc.astype(bf16)


def _proj_qk(x2d, g_mix, w_in_bf, qk_gain):
    m = x2d.shape[0]
    return pl.pallas_call(
        _proj_qk_kernel,
        grid=(m // PROJ_TM, 2 * ATT_WIDTH // PROJ_TN),
        in_specs=[
            pl.BlockSpec((PROJ_TM, D_MODEL), lambda i, j: (i, 0)),
            pl.BlockSpec((1, D_MODEL), lambda i, j: (0, 0)),
            pl.BlockSpec((D_MODEL, PROJ_TN), lambda i, j: (0, j)),
            pl.BlockSpec((2, LANES), lambda i, j: (0, 0)),
        ],
        out_specs=[pl.BlockSpec((PROJ_TM, PROJ_TN), lambda i, j: (i, j)),
                   pl.BlockSpec((PROJ_TM, D_MODEL), lambda i, j: (i, 0))],
        out_shape=[jax.ShapeDtypeStruct((m, 2 * ATT_WIDTH), bf16),
                   jax.ShapeDtypeStruct((m, D_MODEL), bf16)],
        compiler_params=_cparams(("parallel", "arbitrary")),
        name="proj_qk",
    )(x2d, g_mix, w_in_bf, qk_gain)


def _proj_act(h, w_in_bf, col0, width, sigmoid, name):
    m = h.shape[0]
    blk0 = col0 // PROJ_TN
    return pl.pallas_call(
        functools.partial(_proj_act_kernel, sigmoid=sigmoid),
        grid=(m // PROJ_TM, width // PROJ_TN),
        in_specs=[
            pl.BlockSpec((PROJ_TM, D_MODEL), lambda i, j: (i, 0)),
            pl.BlockSpec((D_MODEL, PROJ_TN), lambda i, j: (0, j + blk0)),
        ],
        out_specs=pl.BlockSpec((PROJ_TM, PROJ_TN), lambda i, j: (i, j)),
        out_shape=jax.ShapeDtypeStruct((m, width), bf16),
        compiler_params=_cparams(("parallel", "arbitrary")),
        name=name,
    )(h, w_in_bf)


ATT_QROWS = 8
ATT_TQ = ATT_QROWS * GRID_W
ATT_WROWS = ATT_QROWS + WIN_ROWS - 1
ATT_TK = ATT_WROWS * GRID_W
ATT_NKEY = WIN_ROWS * GRID_W


def _attn_window_start(b):
    return jnp.clip(ATT_QROWS * b - WIN_ROWS // 2, 0, ROWS - ATT_WROWS)


def _attn_kernel(q_ref, k_ref, v_ref, bias_ref, o_ref, s_scr, e_scr):
    b = pl.program_id(1)
    ws = _attn_window_start(b)
    lane = lax.broadcasted_iota(jnp.int32, (1, LANES), 1)
    lo = lane < HEAD_DIM

    def row_body(i, carry):
        r = ATT_QROWS * b + i
        rstart = jnp.clip(r - WIN_ROWS // 2, 0, ROWS - WIN_ROWS)
        off = pl.multiple_of((rstart - ws) * GRID_W, GRID_W)
        ro_base = rstart - r + (WIN_ROWS - 1)
        qoff = pl.multiple_of(i * GRID_W, GRID_W)
        tiles = [slice(t * LANES, (t + 1) * LANES) for t in range(ATT_NKEY // LANES)]

        row_max = []
        for p in range(N_PAIRS):
            cols = slice(p * LANES, (p + 1) * LANES)
            qp = q_ref[pl.ds(qoff, GRID_W), cols]
            zero = jnp.zeros_like(qp)
            lhs = jnp.concatenate([jnp.where(lo, qp, zero), jnp.where(lo, zero, qp)], axis=0)
            kw = k_ref[pl.ds(off, ATT_NKEY), cols]
            s = lax.dot_general(lhs, kw, (((1,), (1,)), ((), ())),
                                preferred_element_type=f32)
            mx = None
            for t, tile in enumerate(tiles):
                st = s[:, tile] + bias_ref[p, ro_base + 2 * t]
                s_scr[p, :, tile] = st
                mx = st if mx is None else jnp.maximum(mx, st)
            row_max.append(jnp.max(mx, axis=-1, keepdims=True))

        row_sum = []
        for p in range(N_PAIRS):
            acc = None
            for tile in tiles:
                e = jnp.exp(s_scr[p, :, tile] - row_max[p])
                e_scr[p, :, tile] = e.astype(bf16)
                acc = e if acc is None else acc + e
            row_sum.append(jnp.sum(acc, axis=-1, keepdims=True))

        for p in range(N_PAIRS):
            cols = slice(p * LANES, (p + 1) * LANES)
            vw = v_ref[pl.ds(off, ATT_NKEY), cols]
            o = jnp.dot(e_scr[p], vw, preferred_element_type=f32) / row_sum[p]
            o_ref[pl.ds(qoff, GRID_W), cols] = jnp.where(
                lo, o[:GRID_W], o[GRID_W:]).astype(bf16)
        return carry

    lax.fori_loop(0, ATT_QROWS, row_body, 0)


def _attn_bias_table(rpb):
    c = jnp.arange(GRID_W)
    cstart = jnp.clip(c - WIN_COLS // 2, 0, GRID_W - WIN_COLS)
    kc = jnp.arange(GRID_W)
    valid = (kc[None, :] >= cstart[:, None]) & (kc[None, :] < cstart[:, None] + WIN_COLS)
    idx = jnp.clip(kc[None, :] - c[:, None] + (WIN_COLS - 1), 0, 2 * WIN_COLS - 2)
    t = jnp.where(valid[None, None], rpb.astype(f32)[:, :, idx], NEG_BIAS)
    t = t.reshape(N_PAIRS, 2, 2 * WIN_ROWS - 1, GRID_W, GRID_W)
    t = t.transpose(0, 2, 1, 3, 4).reshape(N_PAIRS, 2 * WIN_ROWS - 1, 2 * GRID_W, GRID_W)
    return jnp.concatenate([t[:, :-1], t[:, 1:]], axis=-1)


def _attn(zqk, zvf, bias_tab, batch):
    m = zqk.shape[0]
    nblk = ROWS // ATT_QROWS

    def kv_map(col):
        def index_map(s, b):
            return ((s * ROWS + _attn_window_start(b)) * GRID_W, col)
        return index_map

    return pl.pallas_call(
        _attn_kernel,
        grid=(batch, nblk),
        in_specs=[
            pl.BlockSpec((ATT_TQ, ATT_WIDTH), lambda s, b: (s * nblk + b, 0)),
            pl.BlockSpec((pl.Element(ATT_TK), pl.Element(ATT_WIDTH)), kv_map(ATT_WIDTH)),
            pl.BlockSpec((pl.Element(ATT_TK), pl.Element(ATT_WIDTH)), kv_map(0)),
            pl.BlockSpec(bias_tab.shape, lambda s, b: (0, 0, 0, 0)),
        ],
        out_specs=pl.BlockSpec((ATT_TQ, ATT_WIDTH), lambda s, b: (s * nblk + b, 0)),
        out_shape=jax.ShapeDtypeStruct((m, ATT_WIDTH), bf16),
        compiler_params=_cparams(("parallel", "arbitrary")),
        scratch_shapes=[pltpu.VMEM((N_PAIRS, 2 * GRID_W, ATT_NKEY), f32),
                        pltpu.VMEM((N_PAIRS, 2 * GRID_W, ATT_NKEY), bf16)],
        name="attn",
    )(zqk, zqk, zvf, bias_tab)


DFT1_MC = 2
DFT2_SL = 4
DFT3_KB = 4
F_COL_BLOCK = ATT_WIDTH // FOURIER_WIDTH


def _dft1_kernel(m_ref, x_ref, o_ref):
    for c in range(DFT1_MC):
        rows = slice(c * CHUNK, (c + 1) * CHUNK)
        xin = x_ref[0, :, rows, :].reshape(R1 * CHUNK, FOURIER_WIDTH)
        res = jnp.dot(m_ref[c], xin, preferred_element_type=f32)
        o_ref[0, :, :, rows, :] = res.reshape(2, R1, CHUNK, FOURIER_WIDTH).astype(bf16)


def _dft1(zvf, m1, batch):
    zv = zvf.reshape(batch, R1, 256, ATT_WIDTH + FOURIER_WIDTH)
    mrows = DFT1_MC * CHUNK
    return pl.pallas_call(
        _dft1_kernel,
        grid=(batch, 256 // mrows),
        in_specs=[
            pl.BlockSpec((DFT1_MC, 2 * R1 * CHUNK, R1 * CHUNK), lambda s, c: (c, 0, 0)),
            pl.BlockSpec((1, R1, mrows, FOURIER_WIDTH), lambda s, c: (s, 0, c, F_COL_BLOCK)),
        ],
        out_specs=pl.BlockSpec((1, 2, R1, mrows, FOURIER_WIDTH), lambda s, c: (s, 0, 0, c, 0)),
        out_shape=jax.ShapeDtypeStruct((batch, 2, R1, 256, FOURIER_WIDTH), bf16),
        compiler_params=_cparams(("parallel", "arbitrary")),
        name="dft1",
    )(m1, zv)


def _dft2_kernel(m_ref, x_ref, o_ref):
    for sl in range(DFT2_SL):
        rows = slice(sl * 256, (sl + 1) * 256)
        xin = x_ref[0, :, rows, :].reshape(2 * 256, FOURIER_WIDTH)
        res = jnp.dot(m_ref[...], xin, preferred_element_type=f32)
        o_ref[0, :, rows, :] = res.reshape(2, 256, FOURIER_WIDTH).astype(bf16)


def _dft2(y1, m2, batch):
    yv = y1.reshape(batch, 2, SEQ, FOURIER_WIDTH)
    rows = DFT2_SL * 256
    spec = pl.BlockSpec((1, 2, rows, FOURIER_WIDTH), lambda s, c: (s, 0, c, 0))
    return pl.pallas_call(
        _dft2_kernel,
        grid=(batch, SEQ // rows),
        in_specs=[pl.BlockSpec(m2.shape, lambda s, c: (0, 0)), spec],
        out_specs=spec,
        out_shape=jax.ShapeDtypeStruct((batch, 2, SEQ, FOURIER_WIDTH), bf16),
        compiler_params=_cparams(("parallel", "arbitrary")),
        name="dft2",
    )(m2, yv)


def _dft3_kernel(m_ref, x_ref, o_ref):
    for kk in range(DFT3_KB):
        xin = x_ref[0, :, :, kk, :, :].reshape(2 * CHUNK * R3, FOURIER_WIDTH)
        res = jnp.dot(m_ref[...], xin, preferred_element_type=f32)
        o_ref[0, :, :, kk, :, :] = res.reshape(2, R3, CHUNK, FOURIER_WIDTH).astype(bf16)


def _dft3(y2, m3, batch):
    yv = y2.reshape(batch, 2, R1, R2, R3, FOURIER_WIDTH)
    return pl.pallas_call(
        _dft3_kernel,
        grid=(batch, R1 // CHUNK, R2 // DFT3_KB),
        in_specs=[
            pl.BlockSpec(m3.shape, lambda s, h, c: (0, 0)),
            pl.BlockSpec((1, 2, CHUNK, DFT3_KB, R3, FOURIER_WIDTH),
                         lambda s, h, c: (s, 0, h, c, 0, 0)),
        ],
        out_specs=pl.BlockSpec((1, 2, R3, DFT3_KB, CHUNK, FOURIER_WIDTH),
                               lambda s, h, c: (s, 0, 0, c, h, 0)),
        out_shape=jax.ShapeDtypeStruct((batch, 2, R3, R2, R1, FOURIER_WIDTH), bf16),
        compiler_params=_cparams(("parallel", "arbitrary", "arbitrary")),
        name="dft3",
    )(m3, yv)


MERGE_TM = 512


def _merge_kernel(p_ref, att_ref, sa_ref, sb_ref, x_ref, cc_ref, sc_ref,
                  wfo_ref, wao_ref, wout_ref, o_ref):
    fm = []
    for g in range(N_FGROUPS):
        cols = slice(g * FGROUP_DIM, (g + 1) * FGROUP_DIM)
        a = jnp.dot(p_ref[0, 0, :, cols], cc_ref[...], preferred_element_type=f32)
        a = a + jnp.dot(p_ref[0, 1, :, cols], sc_ref[...], preferred_element_type=f32)
        fm.append(a.astype(bf16))
    fm = jnp.concatenate(fm, axis=1)
    ya = jnp.dot(fm, wfo_ref[...], preferred_element_type=f32)
    yb = jnp.dot(att_ref[...], wao_ref[...], preferred_element_type=f32)
    merged = sa_ref[...].astype(f32) * ya + sb_ref[...].astype(f32) * yb
    o_ref[...] = x_ref[...] + jnp.dot(merged.astype(bf16), wout_ref[...],
                                      preferred_element_type=f32)


def _merge(pz, att, zg, x2d, cc, sc, wfo, wao, wout, batch):
    m = x2d.shape[0]
    per_seq = SEQ // MERGE_TM
    pv = pz.reshape(batch, 2, SEQ, FOURIER_WIDTH)
    const = lambda shape: pl.BlockSpec(shape, lambda i: (0, 0), pipeline_mode=pl.Buffered(1))
    return pl.pallas_call(
        _merge_kernel,
        grid=(m // MERGE_TM,),
        in_specs=[
            pl.BlockSpec((1, 2, MERGE_TM, FOURIER_WIDTH),
                         lambda i: (i // per_seq, 0, i % per_seq, 0)),
            pl.BlockSpec((MERGE_TM, ATT_WIDTH), lambda i: (i, 0)),
            pl.BlockSpec((MERGE_TM, D_MODEL), lambda i: (i, 0)),
            pl.BlockSpec((MERGE_TM, D_MODEL), lambda i: (i, 1)),
            pl.BlockSpec((MERGE_TM, D_MODEL), lambda i: (i, 0)),
            const(cc.shape), const(sc.shape), const(wfo.shape), const(wao.shape), const(wout.shape),
        ],
        out_specs=pl.BlockSpec((MERGE_TM, D_MODEL), lambda i: (i, 0)),
        out_shape=jax.ShapeDtypeStruct((m, D_MODEL), f32),
        compiler_params=_cparams(("parallel",)),
        name="merge",
    )(pv, att, zg, zg, x2d, cc, sc, wfo, wao, wout)


MLP_TM = 512
MLP_TF = 512


def _mlp_kernel(x_ref, g_ref, w1_ref, w2_ref, o_ref, h_ref):
    j = pl.program_id(1)

    @pl.when(j == 0)
    def _():
        x = x_ref[...]
        h_ref[...] = _rmsnorm_rows(x, g_ref[...]).astype(bf16)
        o_ref[...] = x

    u = jnp.maximum(jnp.dot(h_ref[...], w1_ref[...], preferred_element_type=f32), 0.0)
    o_ref[...] += jnp.dot((u * u).astype(bf16), w2_ref[...], preferred_element_type=f32)


def _mlp(x1, g_mlp, w1_bf, w2_bf):
    m = x1.shape[0]
    return pl.pallas_call(
        _mlp_kernel,
        grid=(m // MLP_TM, D_FF // MLP_TF),
        in_specs=[
            pl.BlockSpec((MLP_TM, D_MODEL), lambda i, j: (i, 0)),
            pl.BlockSpec((1, D_MODEL), lambda i, j: (0, 0)),
            pl.BlockSpec((D_MODEL, MLP_TF), lambda i, j: (0, j)),
            pl.BlockSpec((MLP_TF, D_MODEL), lambda i, j: (j, 0)),
        ],
        out_specs=pl.BlockSpec((MLP_TM, D_MODEL), lambda i, j: (i, 0)),
        out_shape=jax.ShapeDtypeStruct((m, D_MODEL), f32),
        scratch_shapes=[pltpu.VMEM((MLP_TM, D_MODEL), bf16)],
        compiler_params=_cparams(("parallel", "arbitrary")),
        name="mlp",
    )(x1, g_mlp, w1_bf, w2_bf)


def _encoder_block(x, consts, params):
    batch, seq, _ = x.shape
    assert seq == SEQ
    m1, m2, m3, cc, sc = consts
    g_mix, w_in, qk_gain, bias_tab, w_fo, w_ao, w_out, g_mlp, w1, w2 = params
    x2d = x.reshape(batch * seq, D_MODEL)
    zqk, h = _proj_qk(x2d, g_mix, w_in, qk_gain)
    zvf = _proj_act(h, w_in, 2 * ATT_WIDTH, ATT_WIDTH + FOURIER_WIDTH, False, "proj_vf")
    zg = _proj_act(h, w_in, 3 * ATT_WIDTH + FOURIER_WIDTH, 2 * D_MODEL, True, "proj_gate")
    att = _attn(zqk, zvf, bias_tab, batch)
    y1 = _dft1(zvf, m1, batch)
    y2 = _dft2(y1, m2, batch)
    pz = _dft3(y2, m3, batch)
    x1 = _merge(pz, att, zg, x2d, cc, sc, w_fo, w_ao, w_out, batch)
    out = _mlp(x1, g_mlp, w1, w2)
    return out.reshape(batch, seq, D_MODEL)


def kernel(x_prompt, x_sample, g_mix, w_in, q_norm, k_norm, rpb, w_fo, w_ao, w_out, g_mlp, w1, w2):
    cc_np, sc_np = _channel_matrices()
    consts = tuple(
        jnp.asarray(a, f32).astype(bf16)
        for a in (_stage1_matrices(), _stage2_matrix(), _stage3_matrix(), cc_np, sc_np))
    y_prompt, y_sample = x_prompt, x_sample
    for l in range(g_mix.shape[0]):
        qk_gain = jnp.stack([
            jnp.tile(q_norm[l].astype(f32), 2) * (HEAD_DIM ** -0.5),
            jnp.tile(k_norm[l].astype(f32), 2),
        ])
        params = (
            g_mix[l].astype(f32).reshape(1, D_MODEL), w_in[l].astype(bf16), qk_gain,
            _attn_bias_table(rpb[l]), w_fo[l].astype(bf16), w_ao[l].astype(bf16),
            w_out[l].astype(bf16), g_mlp[l].astype(f32).reshape(1, D_MODEL),
            w1[l].astype(bf16), w2[l].astype(bf16),
        )
        y_prompt = _encoder_block(y_prompt, consts, params)
        y_sample = _encoder_block(y_sample, consts, params)
    return (y_prompt, y_sample)
```

```python
import functools

import numpy as np
import jax
import jax.numpy as jnp
from jax import lax
from jax.experimental import pallas as pl
from jax.experimental.pallas import tpu as pltpu

D_MODEL = 2048
GRID_W = 64
N_HEADS = 16
HEAD_DIM = 64
ATT_WIDTH = N_HEADS * HEAD_DIM
N_FGROUPS = 4
FGROUP_DIM = 256
FOURIER_WIDTH = N_FGROUPS * FGROUP_DIM
WIN_ROWS = 8
WIN_COLS = 16
D_FF = 4 * D_MODEL
IN_WIDTH = 3 * ATT_WIDTH + FOURIER_WIDTH + 2 * D_MODEL
EPS = 1e-6

SEQ = 8192
ROWS = SEQ // GRID_W
LANES = 128
N_PAIRS = N_HEADS // 2
NEG_BIAS = -1e30

R1, R2, R3 = 32, 16, 16
CHUNK = 16

VMEM_LIMIT = 56 * 1024 * 1024

bf16 = jnp.bfloat16
f32 = jnp.float32


def _cparams(sem):
    return pltpu.CompilerParams(dimension_semantics=sem, vmem_limit_bytes=VMEM_LIMIT)


def _stage1_matrices():
    mc = np.arange(256 // CHUNK)[:, None, None, None]
    k1 = np.arange(R1)[None, :, None, None]
    ml = np.arange(CHUNK)[None, None, :, None]
    n1 = np.arange(R1)[None, None, None, :]
    theta = 2.0 * np.pi * (((256 * n1 + 16 * mc + ml) * k1) % SEQ) / SEQ
    eye = np.eye(CHUNK)
    scale = 1.0 / np.sqrt(R1)
    out = np.zeros((256 // CHUNK, 2, R1, CHUNK, R1, CHUNK), np.float64)
    for part, fn in enumerate((np.cos, lambda t: -np.sin(t))):
        vals = fn(theta) * scale
        out[:, part] = vals[:, :, :, :, None] * eye[None, None, :, None, :]
    return out.reshape(256 // CHUNK, 2 * R1 * CHUNK, R1 * CHUNK)


def _complex_block(cos, sin):
    return np.block([[cos, sin], [-sin, cos]])


def _stage2_matrix():
    k2 = np.arange(R2)[:, None, None, None]
    n3o = np.arange(R3)[None, :, None, None]
    n2 = np.arange(R2)[None, None, :, None]
    n3 = np.arange(R3)[None, None, None, :]
    phi = 2.0 * np.pi * (((16 * n2 + n3) * k2) % 256) / 256
    delta = (n3o == n3).astype(np.float64)
    scale = 1.0 / np.sqrt(R2)
    cos = (np.cos(phi) * delta * scale).reshape(R2 * R3, R2 * R3)
    sin = (np.sin(phi) * delta * scale).reshape(R2 * R3, R2 * R3)
    return _complex_block(cos, sin)


def _stage3_matrix():
    k3 = np.arange(R3)[:, None, None, None]
    k1o = np.arange(CHUNK)[None, :, None, None]
    k1 = np.arange(CHUNK)[None, None, :, None]
    n3 = np.arange(R3)[None, None, None, :]
    psi = 2.0 * np.pi * ((n3 * k3) % R3) / R3
    delta = (k1o == k1).astype(np.float64)
    scale = 1.0 / np.sqrt(R3)
    cos = (np.cos(psi) * delta * scale).reshape(R3 * CHUNK, CHUNK * R3)
    sin = (np.sin(psi) * delta * scale).reshape(R3 * CHUNK, CHUNK * R3)
    return _complex_block(cos, sin)


def _channel_matrices():
    c = np.arange(FGROUP_DIM)
    ang = 2.0 * np.pi * ((c[:, None] * c[None, :]) % FGROUP_DIM) / FGROUP_DIM
    scale = 1.0 / np.sqrt(FGROUP_DIM)
    return np.cos(ang) * scale, np.sin(ang) * scale


PROJ_TM = 1024
PROJ_TN = 1024


def _rmsnorm_rows(x, g):
    ms = jnp.mean(x * x, axis=-1, keepdims=True)
    return x * lax.rsqrt(ms + EPS) * g


PROJ_CH = 256


def _proj_qk_kernel(x_ref, g_ref, w_ref, gain_ref, o_ref, h_ref):
    j = pl.program_id(1)

    @pl.when(j == 0)
    def _():
        h_ref[...] = _rmsnorm_rows(x_ref[...], g_ref[...]).astype(bf16)

    gain = gain_ref[pl.ds(j, 1), :]
    lane = lax.broadcasted_iota(jnp.int32, (1, LANES), 1)
    lo = lane < HEAD_DIM
    h = h_ref[...]
    for c in range(PROJ_TN // PROJ_CH):
        acc = jnp.dot(h, w_ref[:, c * PROJ_CH:(c + 1) * PROJ_CH], preferred_element_type=f32)
        for p in range(PROJ_CH // LANES):
            a = acc[:, p * LANES:(p + 1) * LANES]
            sq = a * a
            s0 = jnp.sum(jnp.where(lo, sq, 0.0), axis=-1, keepdims=True)
            s1 = jnp.sum(jnp.where(lo, 0.0, sq), axis=-1, keepdims=True)
            r0 = lax.rsqrt(s0 * (1.0 / HEAD_DIM) + EPS)
            r1 = lax.rsqrt(s1 * (1.0 / HEAD_DIM) + EPS)
            col = c * PROJ_CH + p * LANES
            o_ref[:, col:col + LANES] = (a * jnp.where(lo, r0, r1) * gain).astype(bf16)


def _proj_act_kernel(h_ref, w_ref, o_ref, *, sigmoid):
    h = h_ref[...]
    for c in range(PROJ_TN // PROJ_CH):
        cols = slice(c * PROJ_CH, (c + 1) * PROJ_CH)
        acc = jnp.dot(h, w_ref[:, cols], preferred_element_type=f32)
        if sigmoid:
            acc = 1.0 / (1.0 + jnp.exp(-acc))
        o_ref[:, cols] = acc.astype(bf16)


def _proj_qk(x2d, g_mix, w_in_bf, qk_gain):
    m = x2d.shape[0]
    return pl.pallas_call(
        _proj_qk_kernel,
        grid=(m // PROJ_TM, 2 * ATT_WIDTH // PROJ_TN),
        in_specs=[
            pl.BlockSpec((PROJ_TM, D_MODEL), lambda i, j: (i, 0)),
            pl.BlockSpec((1, D_MODEL), lambda i, j: (0, 0)),
            pl.BlockSpec((D_MODEL, PROJ_TN), lambda i, j: (0, j)),
            pl.BlockSpec((2, LANES), lambda i, j: (0, 0)),
        ],
        out_specs=[pl.BlockSpec((PROJ_TM, PROJ_TN), lambda i, j: (i, j)),
                   pl.BlockSpec((PROJ_TM, D_MODEL), lambda i, j: (i, 0))],
        out_shape=[jax.ShapeDtypeStruct((m, 2 * ATT_WIDTH), bf16),
                   jax.ShapeDtypeStruct((m, D_MODEL), bf16)],
        compiler_params=_cparams(("parallel", "arbitrary")),
        name="proj_qk",
    )(x2d, g_mix, w_in_bf, qk_gain)


def _proj_act(h, w_in_bf, col0, width, sigmoid, name):
    m = h.shape[0]
    blk0 = col0 // PROJ_TN
    return pl.pallas_call(
        functools.partial(_proj_act_kernel, sigmoid=sigmoid),
        grid=(m // PROJ_TM, width // PROJ_TN),
        in_specs=[
            pl.BlockSpec((PROJ_TM, D_MODEL), lambda i, j: (i, 0)),
            pl.BlockSpec((D_MODEL, PROJ_TN), lambda i, j: (0, j + blk0)),
        ],
        out_specs=pl.BlockSpec((PROJ_TM, PROJ_TN), lambda i, j: (i, j)),
        out_shape=jax.ShapeDtypeStruct((m, width), bf16),
        compiler_params=_cparams(("parallel", "arbitrary")),
        name=name,
    )(h, w_in_bf)


ATT_QROWS = 8
ATT_TQ = ATT_QROWS * GRID_W
ATT_WROWS = ATT_QROWS + WIN_ROWS - 1
ATT_TK = ATT_WROWS * GRID_W
ATT_NKEY = WIN_ROWS * GRID_W
ATT_RPI = 2
ATT_UNITS = ATT_RPI * N_PAIRS


def _attn_window_start(b):
    return jnp.clip(ATT_QROWS * b - WIN_ROWS // 2, 0, ROWS - ATT_WROWS)


def _attn_kernel(q_ref, k_ref, v_ref, bias_ref, o_ref, s_scr, e_scr):
    b = pl.program_id(1)
    ws = _attn_window_start(b)
    lane = lax.broadcasted_iota(jnp.int32, (1, LANES), 1)
    lo = lane < HEAD_DIM

    tiles = [slice(t * LANES, (t + 1) * LANES) for t in range(ATT_NKEY // LANES)]

    def rows_body(it, carry):
        units = []
        for sub in range(ATT_RPI):
            i = it * ATT_RPI + sub
            r = ATT_QROWS * b + i
            rstart = jnp.clip(r - WIN_ROWS // 2, 0, ROWS - WIN_ROWS)
            off = pl.multiple_of((rstart - ws) * GRID_W, GRID_W)
            ro_base = rstart - r + (WIN_ROWS - 1)
            qoff = pl.multiple_of(i * GRID_W, GRID_W)
            for p in range(N_PAIRS):
                units.append((p, slice(p * LANES, (p + 1) * LANES), off, ro_base, qoff))

        row_max = []
        for u, (p, cols, off, ro_base, qoff) in enumerate(units):
            qp = q_ref[pl.ds(qoff, GRID_W), cols]
            zero = jnp.zeros_like(qp)
            lhs = jnp.concatenate([jnp.where(lo, qp, zero), jnp.where(lo, zero, qp)], axis=0)
            kw = k_ref[pl.ds(off, ATT_NKEY), cols]
            s = lax.dot_general(lhs, kw, (((1,), (1,)), ((), ())),
                                preferred_element_type=f32)
            mx = None
            for t, tile in enumerate(tiles):
                st = s[:, tile] + bias_ref[p, ro_base + 2 * t]
                s_scr[u, :, tile] = st
                mx = st if mx is None else jnp.maximum(mx, st)
            row_max.append(jnp.max(mx, axis=-1, keepdims=True))

        row_sum = []
        for u in range(len(units)):
            acc = None
            for tile in tiles:
                e = jnp.exp(s_scr[u, :, tile] - row_max[u])
                e_scr[u, :, tile] = e.astype(bf16)
                acc = e if acc is None else acc + e
            row_sum.append(jnp.sum(acc, axis=-1, keepdims=True))

        for u, (p, cols, off, ro_base, qoff) in enumerate(units):
            vw = v_ref[pl.ds(off, ATT_NKEY), cols]
            o = jnp.dot(e_scr[u], vw, preferred_element_type=f32) / row_sum[u]
            o_ref[pl.ds(qoff, GRID_W), cols] = jnp.where(
                lo, o[:GRID_W], o[GRID_W:]).astype(bf16)
        return carry

    lax.fori_loop(0, ATT_QROWS // ATT_RPI, rows_body, 0)


def _attn_bias_table(rpb):
    c = np.arange(GRID_W)
    cstart = np.clip(c - WIN_COLS // 2, 0, GRID_W - WIN_COLS)
    kc = np.arange(GRID_W)
    valid = (kc[None, :] >= cstart[:, None]) & (kc[None, :] < cstart[:, None] + WIN_COLS)
    idx = kc[None, :] - c[:, None] + (WIN_COLS - 1)
    onehot = (valid[None] & (idx[None] == np.arange(2 * WIN_COLS - 1)[:, None, None]))
    t = jnp.einsum('hrj,jck->hrck', rpb.astype(f32), jnp.asarray(onehot, f32),
                   precision=lax.Precision.HIGHEST)
    t = t + jnp.asarray(np.where(valid, 0.0, NEG_BIAS), f32)
    t = t.reshape(N_PAIRS, 2, 2 * WIN_ROWS - 1, GRID_W, GRID_W)
    t = t.transpose(0, 2, 1, 3, 4).reshape(N_PAIRS, 2 * WIN_ROWS - 1, 2 * GRID_W, GRID_W)
    return jnp.concatenate([t[:, :-1], t[:, 1:]], axis=-1)


def _attn(zqk, zvf, bias_tab, batch):
    m = zqk.shape[0]
    nblk = ROWS // ATT_QROWS

    def kv_map(col):
        def index_map(s, b):
            return ((s * ROWS + _attn_window_start(b)) * GRID_W, col)
        return index_map

    return pl.pallas_call(
        _attn_kernel,
        grid=(batch, nblk),
        in_specs=[
            pl.BlockSpec((ATT_TQ, ATT_WIDTH), lambda s, b: (s * nblk + b, 0)),
            pl.BlockSpec((pl.Element(ATT_TK), pl.Element(ATT_WIDTH)), kv_map(ATT_WIDTH)),
            pl.BlockSpec((pl.Element(ATT_TK), pl.Element(ATT_WIDTH)), kv_map(0)),
            pl.BlockSpec(bias_tab.shape, lambda s, b: (0, 0, 0, 0)),
        ],
        out_specs=pl.BlockSpec((ATT_TQ, ATT_WIDTH), lambda s, b: (s * nblk + b, 0)),
        out_shape=jax.ShapeDtypeStruct((m, ATT_WIDTH), bf16),
        compiler_params=_cparams(("parallel", "arbitrary")),
        scratch_shapes=[pltpu.VMEM((ATT_UNITS, 2 * GRID_W, ATT_NKEY), f32),
                        pltpu.VMEM((ATT_UNITS, 2 * GRID_W, ATT_NKEY), bf16)],
        name="attn",
    )(zqk, zqk, zvf, bias_tab)


DFT1_MC = 2
DFT2_SL = 4
DFT3_KB = 4
F_COL_BLOCK = ATT_WIDTH // FOURIER_WIDTH


def _dft1_kernel(m_ref, x_ref, o_ref):
    for c in range(DFT1_MC):
        rows = slice(c * CHUNK, (c + 1) * CHUNK)
        xin = x_ref[0, :, rows, :].reshape(R1 * CHUNK, FOURIER_WIDTH)
        res = jnp.dot(m_ref[c], xin, preferred_element_type=f32)
        o_ref[0, :, :, rows, :] = res.reshape(2, R1, CHUNK, FOURIER_WIDTH).astype(bf16)


def _dft1(zvf, m1, batch):
    zv = zvf.reshape(batch, R1, 256, ATT_WIDTH + FOURIER_WIDTH)
    mrows = DFT1_MC * CHUNK
    return pl.pallas_call(
        _dft1_kernel,
        grid=(batch, 256 // mrows),
        in_specs=[
            pl.BlockSpec((DFT1_MC, 2 * R1 * CHUNK, R1 * CHUNK), lambda s, c: (c, 0, 0)),
            pl.BlockSpec((1, R1, mrows, FOURIER_WIDTH), lambda s, c: (s, 0, c, F_COL_BLOCK)),
        ],
        out_specs=pl.BlockSpec((1, 2, R1, mrows, FOURIER_WIDTH), lambda s, c: (s, 0, 0, c, 0)),
        out_shape=jax.ShapeDtypeStruct((batch, 2, R1, 256, FOURIER_WIDTH), bf16),
        compiler_params=_cparams(("parallel", "arbitrary")),
        name="dft1",
    )(m1, zv)


def _dft2_kernel(m_ref, x_ref, o_ref):
    for sl in range(DFT2_SL):
        rows = slice(sl * 256, (sl + 1) * 256)
        xin = x_ref[0, :, rows, :].reshape(2 * 256, FOURIER_WIDTH)
        res = jnp.dot(m_ref[...], xin, preferred_element_type=f32)
        o_ref[0, :, rows, :] = res.reshape(2, 256, FOURIER_WIDTH).astype(bf16)


def _dft2(y1, m2, batch):
    yv = y1.reshape(batch, 2, SEQ, FOURIER_WIDTH)
    rows = DFT2_SL * 256
    spec = pl.BlockSpec((1, 2, rows, FOURIER_WIDTH), lambda s, c: (s, 0, c, 0))
    return pl.pallas_call(
        _dft2_kernel,
        grid=(batch, SEQ // rows),
        in_specs=[pl.BlockSpec(m2.shape, lambda s, c: (0, 0)), spec],
        out_specs=spec,
        out_shape=jax.ShapeDtypeStruct((batch, 2, SEQ, FOURIER_WIDTH), bf16),
        compiler_params=_cparams(("parallel", "arbitrary")),
        name="dft2",
    )(m2, yv)


def _dft3_kernel(m_ref, x_ref, o_ref):
    for kk in range(DFT3_KB):
        xin = x_ref[0, :, :, kk, :, :].reshape(2 * CHUNK * R3, FOURIER_WIDTH)
        res = jnp.dot(m_ref[...], xin, preferred_element_type=f32)
        o_ref[0, :, :, kk, :, :] = res.reshape(2, R3, CHUNK, FOURIER_WIDTH).astype(bf16)


def _dft3(y2, m3, batch):
    yv = y2.reshape(batch, 2, R1, R2, R3, FOURIER_WIDTH)
    return pl.pallas_call(
        _dft3_kernel,
        grid=(batch, R1 // CHUNK, R2 // DFT3_KB),
        in_specs=[
            pl.BlockSpec(m3.shape, lambda s, h, c: (0, 0)),
            pl.BlockSpec((1, 2, CHUNK, DFT3_KB, R3, FOURIER_WIDTH),
                         lambda s, h, c: (s, 0, h, c, 0, 0)),
        ],
        out_specs=pl.BlockSpec((1, 2, R3, DFT3_KB, CHUNK, FOURIER_WIDTH),
                               lambda s, h, c: (s, 0, 0, c, h, 0)),
        out_shape=jax.ShapeDtypeStruct((batch, 2, R3, R2, R1, FOURIER_WIDTH), bf16),
        compiler_params=_cparams(("parallel", "arbitrary", "arbitrary")),
        name="dft3",
    )(m3, yv)


MERGE_TM = 512


def _merge_kernel(p_ref, att_ref, sa_ref, sb_ref, x_ref, cc_ref, sc_ref,
                  wfo_ref, wao_ref, wout_ref, o_ref):
    fm = []
    for g in range(N_FGROUPS):
        cols = slice(g * FGROUP_DIM, (g + 1) * FGROUP_DIM)
        a = jnp.dot(p_ref[0, 0, :, cols], cc_ref[...], preferred_element_type=f32)
        a = a + jnp.dot(p_ref[0, 1, :, cols], sc_ref[...], preferred_element_type=f32)
        fm.append(a.astype(bf16))
    fm = jnp.concatenate(fm, axis=1)
    ya = jnp.dot(fm, wfo_ref[...], preferred_element_type=f32)
    yb = jnp.dot(att_ref[...], wao_ref[...], preferred_element_type=f32)
    merged = sa_ref[...].astype(f32) * ya + sb_ref[...].astype(f32) * yb
    o_ref[...] = x_ref[...] + jnp.dot(merged.astype(bf16), wout_ref[...],
                                      preferred_element_type=f32)


def _merge(pz, att, zg, x2d, cc, sc, wfo, wao, wout, batch):
    m = x2d.shape[0]
    per_seq = SEQ // MERGE_TM
    pv = pz.reshape(batch, 2, SEQ, FOURIER_WIDTH)
    const = lambda shape: pl.BlockSpec(shape, lambda i: (0, 0), pipeline_mode=pl.Buffered(1))
    return pl.pallas_call(
        _merge_kernel,
        grid=(m // MERGE_TM,),
        in_specs=[
            pl.BlockSpec((1, 2, MERGE_TM, FOURIER_WIDTH),
                         lambda i: (i // per_seq, 0, i % per_seq, 0)),
            pl.BlockSpec((MERGE_TM, ATT_WIDTH), lambda i: (i, 0)),
            pl.BlockSpec((MERGE_TM, D_MODEL), lambda i: (i, 0)),
            pl.BlockSpec((MERGE_TM, D_MODEL), lambda i: (i, 1)),
            pl.BlockSpec((MERGE_TM, D_MODEL), lambda i: (i, 0)),
            const(cc.shape), const(sc.shape), const(wfo.shape), const(wao.shape), const(wout.shape),
        ],
        out_specs=pl.BlockSpec((MERGE_TM, D_MODEL), lambda i: (i, 0)),
        out_shape=jax.ShapeDtypeStruct((m, D_MODEL), f32),
        compiler_params=_cparams(("parallel",)),
        name="merge",
    )(pv, att, zg, zg, x2d, cc, sc, wfo, wao, wout)


MLP_TM = 512
MLP_TF = 1024
MLP_CH = 256


def _mlp_kernel(x_ref, g_ref, w1_ref, w2_ref, o_ref, h_ref):
    j = pl.program_id(1)

    @pl.when(j == 0)
    def _():
        x = x_ref[...]
        h_ref[...] = _rmsnorm_rows(x, g_ref[...]).astype(bf16)
        o_ref[...] = x

    h = h_ref[...]
    us = []
    for c in range(MLP_TF // MLP_CH):
        u = jnp.dot(h, w1_ref[:, c * MLP_CH:(c + 1) * MLP_CH], preferred_element_type=f32)
        u = jnp.maximum(u, 0.0)
        us.append((u * u).astype(bf16))
    o_ref[...] += jnp.dot(jnp.concatenate(us, axis=1), w2_ref[...], preferred_element_type=f32)


def _mlp(x1, g_mlp, w1_bf, w2_bf):
    m = x1.shape[0]
    return pl.pallas_call(
        _mlp_kernel,
        grid=(m // MLP_TM, D_FF // MLP_TF),
        in_specs=[
            pl.BlockSpec((MLP_TM, D_MODEL), lambda i, j: (i, 0)),
            pl.BlockSpec((1, D_MODEL), lambda i, j: (0, 0)),
            pl.BlockSpec((D_MODEL, MLP_TF), lambda i, j: (0, j)),
            pl.BlockSpec((MLP_TF, D_MODEL), lambda i, j: (j, 0)),
        ],
        out_specs=pl.BlockSpec((MLP_TM, D_MODEL), lambda i, j: (i, 0)),
        out_shape=jax.ShapeDtypeStruct((m, D_MODEL), f32),
        scratch_shapes=[pltpu.VMEM((MLP_TM, D_MODEL), bf16)],
        compiler_params=_cparams(("parallel", "arbitrary")),
        name="mlp",
    )(x1, g_mlp, w1_bf, w2_bf)


def _encoder_block(x, consts, params):
    batch, seq, _ = x.shape
    assert seq == SEQ
    m1, m2, m3, cc, sc = consts
    g_mix, w_in, qk_gain, bias_tab, w_fo, w_ao, w_out, g_mlp, w1, w2 = params
    x2d = x.reshape(batch * seq, D_MODEL)
    zqk, h = _proj_qk(x2d, g_mix, w_in, qk_gain)
    zvf = _proj_act(h, w_in, 2 * ATT_WIDTH, ATT_WIDTH + FOURIER_WIDTH, False, "proj_vf")
    zg = _proj_act(h, w_in, 3 * ATT_WIDTH + FOURIER_WIDTH, 2 * D_MODEL, True, "proj_gate")
    att = _attn(zqk, zvf, bias_tab, batch)
    y1 = _dft1(zvf, m1, batch)
    y2 = _dft2(y1, m2, batch)
    pz = _dft3(y2, m3, batch)
    x1 = _merge(pz, att, zg, x2d, cc, sc, w_fo, w_ao, w_out, batch)
    out = _mlp(x1, g_mlp, w1, w2)
    return out.reshape(batch, seq, D_MODEL)


def kernel(x_prompt, x_sample, g_mix, w_in, q_norm, k_norm, rpb, w_fo, w_ao, w_out, g_mlp, w1, w2):
    cc_np, sc_np = _channel_matrices()
    consts = tuple(
        jnp.asarray(a, f32).astype(bf16)
        for a in (_stage1_matrices(), _stage2_matrix(), _stage3_matrix(), cc_np, sc_np))
    y_prompt, y_sample = x_prompt, x_sample
    for l in range(g_mix.shape[0]):
        qk_gain = jnp.stack([
            jnp.tile(q_norm[l].astype(f32), 2) * (HEAD_DIM ** -0.5),
            jnp.tile(k_norm[l].astype(f32), 2),
        ])
        params = (
            g_mix[l].astype(f32).reshape(1, D_MODEL), w_in[l].astype(bf16), qk_gain,
            _attn_bias_table(rpb[l]), w_fo[l].astype(bf16), w_ao[l].astype(bf16),
            w_out[l].astype(bf16), g_mlp[l].astype(f32).reshape(1, D_MODEL),
            w1[l].astype(bf16), w2[l].astype(bf16),
        )
        y_prompt = _encoder_block(y_prompt, consts, params)
        y_sample = _encoder_block(y_sample, consts, params)
    return (y_prompt, y_sample)
```

```python
import functools

import numpy as np
import jax
import jax.numpy as jnp
from jax import lax
from jax.experimental import pallas as pl
from jax.experimental.pallas import tpu as pltpu

D_MODEL = 2048
GRID_W = 64
N_HEADS = 16
HEAD_DIM = 64
ATT_WIDTH = N_HEADS * HEAD_DIM
N_FGROUPS = 4
FGROUP_DIM = 256
FOURIER_WIDTH = N_FGROUPS * FGROUP_DIM
WIN_ROWS = 8
WIN_COLS = 16
D_FF = 4 * D_MODEL
IN_WIDTH = 3 * ATT_WIDTH + FOURIER_WIDTH + 2 * D_MODEL
EPS = 1e-6

SEQ = 8192
ROWS = SEQ // GRID_W
LANES = 128
N_PAIRS = N_HEADS // 2
NEG_BIAS = -1e30

R1, R2, R3 = 32, 16, 16
CHUNK = 16

VMEM_LIMIT = 56 * 1024 * 1024

bf16 = jnp.bfloat16
f32 = jnp.float32


def _cparams(sem):
    return pltpu.CompilerParams(dimension_semantics=sem, vmem_limit_bytes=VMEM_LIMIT)


def _stage1_matrices():
    mc = np.arange(256 // CHUNK)[:, None, None, None]
    k1 = np.arange(R1)[None, :, None, None]
    ml = np.arange(CHUNK)[None, None, :, None]
    n1 = np.arange(R1)[None, None, None, :]
    theta = 2.0 * np.pi * (((256 * n1 + 16 * mc + ml) * k1) % SEQ) / SEQ
    eye = np.eye(CHUNK)
    scale = 1.0 / np.sqrt(R1)
    out = np.zeros((256 // CHUNK, 2, R1, CHUNK, R1, CHUNK), np.float64)
    for part, fn in enumerate((np.cos, lambda t: -np.sin(t))):
        vals = fn(theta) * scale
        out[:, part] = vals[:, :, :, :, None] * eye[None, None, :, None, :]
    return out.reshape(256 // CHUNK, 2 * R1 * CHUNK, R1 * CHUNK)


def _complex_block(cos, sin):
    return np.block([[cos, sin], [-sin, cos]])


def _stage2_matrix():
    k2 = np.arange(R2)[:, None, None, None]
    n3o = np.arange(R3)[None, :, None, None]
    n2 = np.arange(R2)[None, None, :, None]
    n3 = np.arange(R3)[None, None, None, :]
    phi = 2.0 * np.pi * (((16 * n2 + n3) * k2) % 256) / 256
    delta = (n3o == n3).astype(np.float64)
    scale = 1.0 / np.sqrt(R2)
    cos = (np.cos(phi) * delta * scale).reshape(R2 * R3, R2 * R3)
    sin = (np.sin(phi) * delta * scale).reshape(R2 * R3, R2 * R3)
    return _complex_block(cos, sin)


def _stage3_matrix():
    k3 = np.arange(R3)[:, None, None, None]
    k1o = np.arange(CHUNK)[None, :, None, None]
    k1 = np.arange(CHUNK)[None, None, :, None]
    n3 = np.arange(R3)[None, None, None, :]
    psi = 2.0 * np.pi * ((n3 * k3) % R3) / R3
    delta = (k1o == k1).astype(np.float64)
    scale = 1.0 / np.sqrt(R3)
    cos = (np.cos(psi) * delta * scale).reshape(R3 * CHUNK, CHUNK * R3)
    sin = (np.sin(psi) * delta * scale).reshape(R3 * CHUNK, CHUNK * R3)
    return _complex_block(cos, sin)


def _channel_matrices():
    c = np.arange(FGROUP_DIM)
    ang = 2.0 * np.pi * ((c[:, None] * c[None, :]) % FGROUP_DIM) / FGROUP_DIM
    scale = 1.0 / np.sqrt(FGROUP_DIM)
    return np.cos(ang) * scale, np.sin(ang) * scale


PROJ_TM = 1024
PROJ_TN = 1024


def _rmsnorm_rows(x, g):
    ms = jnp.mean(x * x, axis=-1, keepdims=True)
    return x * lax.rsqrt(ms + EPS) * g


PROJ_CH = 256


def _proj_qk_kernel(x_ref, g_ref, w_ref, gain_ref, o_ref, h_ref):
    j = pl.program_id(1)

    @pl.when(j == 0)
    def _():
        h_ref[...] = _rmsnorm_rows(x_ref[...], g_ref[...]).astype(bf16)

    gain = gain_ref[pl.ds(j, 1), :]
    lane = lax.broadcasted_iota(jnp.int32, (1, LANES), 1)
    lo = lane < HEAD_DIM
    h = h_ref[...]
    for c in range(PROJ_TN // PROJ_CH):
        acc = jnp.dot(h, w_ref[:, c * PROJ_CH:(c + 1) * PROJ_CH], preferred_element_type=f32)
        for p in range(PROJ_CH // LANES):
            a = acc[:, p * LANES:(p + 1) * LANES]
            sq = a * a
            s0 = jnp.sum(jnp.where(lo, sq, 0.0), axis=-1, keepdims=True)
            s1 = jnp.sum(jnp.where(lo, 0.0, sq), axis=-1, keepdims=True)
            r0 = lax.rsqrt(s0 * (1.0 / HEAD_DIM) + EPS)
            r1 = lax.rsqrt(s1 * (1.0 / HEAD_DIM) + EPS)
            col = c * PROJ_CH + p * LANES
            o_ref[:, col:col + LANES] = (a * jnp.where(lo, r0, r1) * gain).astype(bf16)


def _proj_act_kernel(h_ref, w_ref, o_ref, *, sigmoid):
    h = h_ref[...]
    for c in range(PROJ_TN // PROJ_CH):
        cols = slice(c * PROJ_CH, (c + 1) * PROJ_CH)
        acc = jnp.dot(h, w_ref[:, cols], preferred_element_type=f32)
        if sigmoid:
            acc = 0.5 * jnp.tanh(0.5 * acc) + 0.5
        o_ref[:, cols] = acc.astype(bf16)


def _proj_qk(x2d, g_mix, w_in_bf, qk_gain):
    m = x2d.shape[0]
    return pl.pallas_call(
        _proj_qk_kernel,
        grid=(m // PROJ_TM, 2 * ATT_WIDTH // PROJ_TN),
        in_specs=[
            pl.BlockSpec((PROJ_TM, D_MODEL), lambda i, j: (i, 0)),
            pl.BlockSpec((1, D_MODEL), lambda i, j: (0, 0)),
            pl.BlockSpec((D_MODEL, PROJ_TN), lambda i, j: (0, j)),
            pl.BlockSpec((2, LANES), lambda i, j: (0, 0)),
        ],
        out_specs=[pl.BlockSpec((PROJ_TM, PROJ_TN), lambda i, j: (i, j)),
                   pl.BlockSpec((PROJ_TM, D_MODEL), lambda i, j: (i, 0))],
        out_shape=[jax.ShapeDtypeStruct((m, 2 * ATT_WIDTH), bf16),
                   jax.ShapeDtypeStruct((m, D_MODEL), bf16)],
        compiler_params=_cparams(("parallel", "arbitrary")),
        name="proj_qk",
    )(x2d, g_mix, w_in_bf, qk_gain)


def _proj_act(h, w_in_bf, col0, width, sigmoid, name):
    m = h.shape[0]
    blk0 = col0 // PROJ_TN
    return pl.pallas_call(
        functools.partial(_proj_act_kernel, sigmoid=sigmoid),
        grid=(m // PROJ_TM, width // PROJ_TN),
        in_specs=[
            pl.BlockSpec((PROJ_TM, D_MODEL), lambda i, j: (i, 0)),
            pl.BlockSpec((D_MODEL, PROJ_TN), lambda i, j: (0, j + blk0)),
        ],
        out_specs=pl.BlockSpec((PROJ_TM, PROJ_TN), lambda i, j: (i, j)),
        out_shape=jax.ShapeDtypeStruct((m, width), bf16),
        compiler_params=_cparams(("parallel", "arbitrary")),
        name=name,
    )(h, w_in_bf)


ATT_QROWS = 8
ATT_TQ = ATT_QROWS * GRID_W
ATT_WROWS = ATT_QROWS + WIN_ROWS - 1
ATT_TK = ATT_WROWS * GRID_W
ATT_NKEY = WIN_ROWS * GRID_W
ATT_RPI = 2
ATT_UNITS = ATT_RPI * N_PAIRS


def _attn_window_start(b):
    return jnp.clip(ATT_QROWS * b - WIN_ROWS // 2, 0, ROWS - ATT_WROWS)


def _attn_kernel(q_ref, k_ref, v_ref, bias_ref, o_ref, s_scr):
    b = pl.program_id(1)
    ws = _attn_window_start(b)
    lane = lax.broadcasted_iota(jnp.int32, (1, LANES), 1)
    lo = lane < HEAD_DIM

    def rows_body(it, carry):
        units = []
        for sub in range(ATT_RPI):
            i = it * ATT_RPI + sub
            r = ATT_QROWS * b + i
            rstart = jnp.clip(r - WIN_ROWS // 2, 0, ROWS - WIN_ROWS)
            off = pl.multiple_of((rstart - ws) * GRID_W, GRID_W)
            ro_base = rstart - r + (WIN_ROWS - 1)
            qoff = pl.multiple_of(i * GRID_W, GRID_W)
            for p in range(N_PAIRS):
                units.append((p, slice(p * LANES, (p + 1) * LANES), off, ro_base, qoff))

        tiles = [slice(t * LANES, (t + 1) * LANES) for t in range(ATT_NKEY // LANES)]
        row_max = []
        for u, (p, cols, off, ro_base, qoff) in enumerate(units):
            qp = q_ref[pl.ds(qoff, GRID_W), cols]
            zero = jnp.zeros_like(qp)
            lhs = jnp.concatenate([jnp.where(lo, qp, zero), jnp.where(lo, zero, qp)], axis=0)
            kw = k_ref[pl.ds(off, ATT_NKEY), cols]
            s = lax.dot_general(lhs, kw, (((1,), (1,)), ((), ())),
                                preferred_element_type=f32)
            mx = None
            for t, tile in enumerate(tiles):
                st = s[:, tile] + bias_ref[p, ro_base + 2 * t]
                s_scr[u, :, tile] = st
                mx = st if mx is None else jnp.maximum(mx, st)
            row_max.append(jnp.max(mx, axis=-1, keepdims=True))

        for u, (p, cols, off, ro_base, qoff) in enumerate(units):
            acc = None
            o = None
            for t, tile in enumerate(tiles):
                e = jnp.exp(s_scr[u, :, tile] - row_max[u])
                acc = e if acc is None else acc + e
                vt = v_ref[pl.ds(off + t * LANES, LANES), cols]
                ot = jnp.dot(e.astype(bf16), vt, preferred_element_type=f32)
                o = ot if o is None else o + ot
            o = o / jnp.sum(acc, axis=-1, keepdims=True)
            o_ref[pl.ds(qoff, GRID_W), cols] = jnp.where(
                lo, o[:GRID_W], o[GRID_W:]).astype(bf16)
        return carry

    lax.fori_loop(0, ATT_QROWS // ATT_RPI, rows_body, 0)


def _attn_bias_table(rpb):
    c = np.arange(GRID_W)
    cstart = np.clip(c - WIN_COLS // 2, 0, GRID_W - WIN_COLS)
    kc = np.arange(GRID_W)
    valid = (kc[None, :] >= cstart[:, None]) & (kc[None, :] < cstart[:, None] + WIN_COLS)
    idx = kc[None, :] - c[:, None] + (WIN_COLS - 1)
    onehot = (valid[None] & (idx[None] == np.arange(2 * WIN_COLS - 1)[:, None, None]))
    t = jnp.einsum('hrj,jck->hrck', rpb.astype(f32), jnp.asarray(onehot, f32),
                   precision=lax.Precision.HIGHEST)
    t = t + jnp.asarray(np.where(valid, 0.0, NEG_BIAS), f32)
    t = t.reshape(N_PAIRS, 2, 2 * WIN_ROWS - 1, GRID_W, GRID_W)
    t = t.transpose(0, 2, 1, 3, 4).reshape(N_PAIRS, 2 * WIN_ROWS - 1, 2 * GRID_W, GRID_W)
    return jnp.concatenate([t[:, :-1], t[:, 1:]], axis=-1)


def _attn(zqk, zvf, bias_tab, batch):
    m = zqk.shape[0]
    nblk = ROWS // ATT_QROWS

    def kv_map(col):
        def index_map(s, b):
            return ((s * ROWS + _attn_window_start(b)) * GRID_W, col)
        return index_map

    return pl.pallas_call(
        _attn_kernel,
        grid=(batch, nblk),
        in_specs=[
            pl.BlockSpec((ATT_TQ, ATT_WIDTH), lambda s, b: (s * nblk + b, 0)),
            pl.BlockSpec((pl.Element(ATT_TK), pl.Element(ATT_WIDTH)), kv_map(ATT_WIDTH)),
            pl.BlockSpec((pl.Element(ATT_TK), pl.Element(ATT_WIDTH)), kv_map(0)),
            pl.BlockSpec(bias_tab.shape, lambda s, b: (0, 0, 0, 0)),
        ],
        out_specs=pl.BlockSpec((ATT_TQ, ATT_WIDTH), lambda s, b: (s * nblk + b, 0)),
        out_shape=jax.ShapeDtypeStruct((m, ATT_WIDTH), bf16),
        compiler_params=_cparams(("parallel", "arbitrary")),
        scratch_shapes=[pltpu.VMEM((ATT_UNITS, 2 * GRID_W, ATT_NKEY), f32)],
        name="attn",
    )(zqk, zqk, zvf, bias_tab)


def _dft_kernel(m1_ref, m2_ref, m3_ref, x_ref, o_ref, y_ref):
    for c in range(256 // CHUNK):
        rows = slice(c * CHUNK, (c + 1) * CHUNK)
        xin = x_ref[0, :, rows, :].reshape(R1 * CHUNK, FGROUP_DIM)
        res = jnp.dot(m1_ref[c], xin, preferred_element_type=f32)
        y_ref[:, :, c, :, :] = res.reshape(2, R1, CHUNK, FGROUP_DIM).astype(bf16)
    for k1 in range(R1):
        xin = y_ref[:, k1].reshape(2 * R2 * R3, FGROUP_DIM)
        res = jnp.dot(m2_ref[...], xin, preferred_element_type=f32)
        y_ref[:, k1] = res.reshape(2, R2, R3, FGROUP_DIM).astype(bf16)
    for h in range(R1 // CHUNK):
        k1s = slice(h * CHUNK, (h + 1) * CHUNK)
        for k2 in range(R2):
            xin = y_ref[:, k1s, k2, :, :].reshape(2 * CHUNK * R3, FGROUP_DIM)
            res = jnp.dot(m3_ref[...], xin, preferred_element_type=f32)
            o_ref[0, :, :, k2, k1s, :] = res.reshape(2, R3, CHUNK, FGROUP_DIM).astype(bf16)


def _dft(zvf, m1, m2, m3, batch):
    zv = zvf.reshape(batch, R1, 256, ATT_WIDTH + FOURIER_WIDTH)
    f_blk0 = ATT_WIDTH // FGROUP_DIM
    const = lambda a: pl.BlockSpec(a.shape, lambda s, g: (0,) * a.ndim,
                                   pipeline_mode=pl.Buffered(1))
    return pl.pallas_call(
        _dft_kernel,
        grid=(batch, N_FGROUPS),
        in_specs=[
            const(m1), const(m2), const(m3),
            pl.BlockSpec((1, R1, 256, FGROUP_DIM), lambda s, g: (s, 0, 0, f_blk0 + g)),
        ],
        out_specs=pl.BlockSpec((1, 2, R3, R2, R1, FGROUP_DIM), lambda s, g: (s, 0, 0, 0, 0, g)),
        out_shape=jax.ShapeDtypeStruct((batch, 2, R3, R2, R1, FOURIER_WIDTH), bf16),
        scratch_shapes=[pltpu.VMEM((2, R1, R2, R3, FGROUP_DIM), bf16)],
        compiler_params=_cparams(("parallel", "arbitrary")),
        name="dft",
    )(m1, m2, m3, zv)


MERGE_TM = 512


def _merge_kernel(p_ref, att_ref, sa_ref, sb_ref, x_ref, cc_ref, sc_ref,
                  wfo_ref, wao_ref, wout_ref, o_ref):
    fm = []
    for g in range(N_FGROUPS):
        cols = slice(g * FGROUP_DIM, (g + 1) * FGROUP_DIM)
        a = jnp.dot(p_ref[0, 0, :, cols], cc_ref[...], preferred_element_type=f32)
        a = a + jnp.dot(p_ref[0, 1, :, cols], sc_ref[...], preferred_element_type=f32)
        fm.append(a.astype(bf16))
    fm = jnp.concatenate(fm, axis=1)
    ya = jnp.dot(fm, wfo_ref[...], preferred_element_type=f32)
    yb = jnp.dot(att_ref[...], wao_ref[...], preferred_element_type=f32)
    merged = sa_ref[...].astype(f32) * ya + sb_ref[...].astype(f32) * yb
    o_ref[...] = x_ref[...] + jnp.dot(merged.astype(bf16), wout_ref[...],
                                      preferred_element_type=f32)


def _merge(pz, att, zg, x2d, cc, sc, wfo, wao, wout, batch):
    m = x2d.shape[0]
    per_seq = SEQ // MERGE_TM
    pv = pz.reshape(batch, 2, SEQ, FOURIER_WIDTH)
    const = lambda shape: pl.BlockSpec(shape, lambda i: (0, 0), pipeline_mode=pl.Buffered(1))
    return pl.pallas_call(
        _merge_kernel,
        grid=(m // MERGE_TM,),
        in_specs=[
            pl.BlockSpec((1, 2, MERGE_TM, FOURIER_WIDTH),
                         lambda i: (i // per_seq, 0, i % per_seq, 0)),
            pl.BlockSpec((MERGE_TM, ATT_WIDTH), lambda i: (i, 0)),
            pl.BlockSpec((MERGE_TM, D_MODEL), lambda i: (i, 0)),
            pl.BlockSpec((MERGE_TM, D_MODEL), lambda i: (i, 1)),
            pl.BlockSpec((MERGE_TM, D_MODEL), lambda i: (i, 0)),
            const(cc.shape), const(sc.shape), const(wfo.shape), const(wao.shape), const(wout.shape),
        ],
        out_specs=pl.BlockSpec((MERGE_TM, D_MODEL), lambda i: (i, 0)),
        out_shape=jax.ShapeDtypeStruct((m, D_MODEL), f32),
        compiler_params=_cparams(("parallel",)),
        name="merge",
    )(pv, att, zg, zg, x2d, cc, sc, wfo, wao, wout)


MLP_TM = 512
MLP_TF = 1024
MLP_CH = 256


def _mlp_kernel(x_ref, g_ref, w1_ref, w2_ref, o_ref, h_ref):
    j = pl.program_id(1)

    @pl.when(j == 0)
    def _():
        x = x_ref[...]
        h_ref[...] = _rmsnorm_rows(x, g_ref[...]).astype(bf16)
        o_ref[...] = x

    h = h_ref[...]
    us = []
    for c in range(MLP_TF // MLP_CH):
        u = jnp.dot(h, w1_ref[:, c * MLP_CH:(c + 1) * MLP_CH], preferred_element_type=f32)
        u = jnp.maximum(u, 0.0)
        us.append((u * u).astype(bf16))
    o_ref[...] += jnp.dot(jnp.concatenate(us, axis=1), w2_ref[...], preferred_element_type=f32)


def _mlp(x1, g_mlp, w1_bf, w2_bf):
    m = x1.shape[0]
    return pl.pallas_call(
        _mlp_kernel,
        grid=(m // MLP_TM, D_FF // MLP_TF),
        in_specs=[
            pl.BlockSpec((MLP_TM, D_MODEL), lambda i, j: (i, 0)),
            pl.BlockSpec((1, D_MODEL), lambda i, j: (0, 0)),
            pl.BlockSpec((D_MODEL, MLP_TF), lambda i, j: (0, j)),
            pl.BlockSpec((MLP_TF, D_MODEL), lambda i, j: (j, 0)),
        ],
        out_specs=pl.BlockSpec((MLP_TM, D_MODEL), lambda i, j: (i, 0)),
        out_shape=jax.ShapeDtypeStruct((m, D_MODEL), f32),
        scratch_shapes=[pltpu.VMEM((MLP_TM, D_MODEL), bf16)],
        compiler_params=_cparams(("parallel", "arbitrary")),
        name="mlp",
    )(x1, g_mlp, w1_bf, w2_bf)


def _encoder_block(x, consts, params):
    batch, seq, _ = x.shape
    assert seq == SEQ
    m1, m2, m3, cc, sc = consts
    g_mix, w_in, qk_gain, bias_tab, w_fo, w_ao, w_out, g_mlp, w1, w2 = params
    x2d = x.reshape(batch * seq, D_MODEL)
    zqk, h = _proj_qk(x2d, g_mix, w_in, qk_gain)
    zvf = _proj_act(h, w_in, 2 * ATT_WIDTH, ATT_WIDTH + FOURIER_WIDTH, False, "proj_vf")
    zg = _proj_act(h, w_in, 3 * ATT_WIDTH + FOURIER_WIDTH, 2 * D_MODEL, True, "proj_gate")
    att = _attn(zqk, zvf, bias_tab, batch)
    pz = _dft(zvf, m1, m2, m3, batch)
    x1 = _merge(pz, att, zg, x2d, cc, sc, w_fo, w_ao, w_out, batch)
    out = _mlp(x1, g_mlp, w1, w2)
    return out.reshape(batch, seq, D_MODEL)


def kernel(x_prompt, x_sample, g_mix, w_in, q_norm, k_norm, rpb, w_fo, w_ao, w_out, g_mlp, w1, w2):
    cc_np, sc_np = _channel_matrices()
    consts = tuple(
        jnp.asarray(a, f32).astype(bf16)
        for a in (_stage1_matrices(), _stage2_matrix(), _stage3_matrix(), cc_np, sc_np))
    y_prompt, y_sample = x_prompt, x_sample
    for l in range(g_mix.shape[0]):
        qk_gain = jnp.stack([
            jnp.tile(q_norm[l].astype(f32), 2) * (HEAD_DIM ** -0.5),
            jnp.tile(k_norm[l].astype(f32), 2),
        ])
        params = (
            g_mix[l].astype(f32).reshape(1, D_MODEL), w_in[l].astype(bf16), qk_gain,
            _attn_bias_table(rpb[l]), w_fo[l].astype(bf16), w_ao[l].astype(bf16),
            w_out[l].astype(bf16), g_mlp[l].astype(f32).reshape(1, D_MODEL),
            w1[l].astype(bf16), w2[l].astype(bf16),
        )
        y_prompt = _encoder_block(y_prompt, consts, params)
        y_sample = _encoder_block(y_sample, consts, params)
    return (y_prompt, y_sample)
```

```python
import functools

import numpy as np
import jax
import jax.numpy as jnp
from jax import lax
from jax.experimental import pallas as pl
from jax.experimental.pallas import tpu as pltpu

D_MODEL = 2048
GRID_W = 64
N_HEADS = 16
HEAD_DIM = 64
ATT_WIDTH = N_HEADS * HEAD_DIM
N_FGROUPS = 4
FGROUP_DIM = 256
FOURIER_WIDTH = N_FGROUPS * FGROUP_DIM
WIN_ROWS = 8
WIN_COLS = 16
D_FF = 4 * D_MODEL
IN_WIDTH = 3 * ATT_WIDTH + FOURIER_WIDTH + 2 * D_MODEL
EPS = 1e-6

SEQ = 8192
ROWS = SEQ // GRID_W
LANES = 128
N_PAIRS = N_HEADS // 2
NEG_BIAS = -1e30

R1, R2, R3 = 32, 16, 16
CHUNK = 16

VMEM_LIMIT = 56 * 1024 * 1024

bf16 = jnp.bfloat16
f32 = jnp.float32


def _cparams(sem):
    return pltpu.CompilerParams(dimension_semantics=sem, vmem_limit_bytes=VMEM_LIMIT)


def _stage1_matrices():
    mc = np.arange(256 // CHUNK)[:, None, None, None]
    k1 = np.arange(R1)[None, :, None, None]
    ml = np.arange(CHUNK)[None, None, :, None]
    n1 = np.arange(R1)[None, None, None, :]
    theta = 2.0 * np.pi * (((256 * n1 + 16 * mc + ml) * k1) % SEQ) / SEQ
    eye = np.eye(CHUNK)
    scale = 1.0 / np.sqrt(R1)
    out = np.zeros((256 // CHUNK, 2, R1, CHUNK, R1, CHUNK), np.float64)
    for part, fn in enumerate((np.cos, lambda t: -np.sin(t))):
        vals = fn(theta) * scale
        out[:, part] = vals[:, :, :, :, None] * eye[None, None, :, None, :]
    return out.reshape(256 // CHUNK, 2 * R1 * CHUNK, R1 * CHUNK)


def _complex_block(cos, sin):
    return np.block([[cos, sin], [-sin, cos]])


def _stage2_matrix():
    k2 = np.arange(R2)[:, None, None, None]
    n3o = np.arange(R3)[None, :, None, None]
    n2 = np.arange(R2)[None, None, :, None]
    n3 = np.arange(R3)[None, None, None, :]
    phi = 2.0 * np.pi * (((16 * n2 + n3) * k2) % 256) / 256
    delta = (n3o == n3).astype(np.float64)
    scale = 1.0 / np.sqrt(R2)
    cos = (np.cos(phi) * delta * scale).reshape(R2 * R3, R2 * R3)
    sin = (np.sin(phi) * delta * scale).reshape(R2 * R3, R2 * R3)
    return _complex_block(cos, sin)


def _stage3_matrix():
    k3 = np.arange(R3)[:, None, None, None]
    k1o = np.arange(CHUNK)[None, :, None, None]
    k1 = np.arange(CHUNK)[None, None, :, None]
    n3 = np.arange(R3)[None, None, None, :]
    psi = 2.0 * np.pi * ((n3 * k3) % R3) / R3
    delta = (k1o == k1).astype(np.float64)
    scale = 1.0 / np.sqrt(R3)
    cos = (np.cos(psi) * delta * scale).reshape(R3 * CHUNK, CHUNK * R3)
    sin = (np.sin(psi) * delta * scale).reshape(R3 * CHUNK, CHUNK * R3)
    return _complex_block(cos, sin)


def _channel_matrices():
    c = np.arange(FGROUP_DIM)
    ang = 2.0 * np.pi * ((c[:, None] * c[None, :]) % FGROUP_DIM) / FGROUP_DIM
    scale = 1.0 / np.sqrt(FGROUP_DIM)
    return np.cos(ang) * scale, np.sin(ang) * scale


PROJ_TM = 1024
PROJ_TN = 1024


def _rmsnorm_rows(x, g):
    ms = jnp.mean(x * x, axis=-1, keepdims=True)
    return x * lax.rsqrt(ms + EPS) * g


PROJ_CH = 256


def _proj_qk_kernel(x_ref, g_ref, w_ref, gain_ref, o_ref, h_ref):
    j = pl.program_id(1)

    @pl.when(j == 0)
    def _():
        h_ref[...] = _rmsnorm_rows(x_ref[...], g_ref[...]).astype(bf16)

    gain = gain_ref[pl.ds(j, 1), :]
    lane = lax.broadcasted_iota(jnp.int32, (1, LANES), 1)
    lo = lane < HEAD_DIM
    h = h_ref[...]
    for c in range(PROJ_TN // PROJ_CH):
        acc = jnp.dot(h, w_ref[:, c * PROJ_CH:(c + 1) * PROJ_CH], preferred_element_type=f32)
        for p in range(PROJ_CH // LANES):
            a = acc[:, p * LANES:(p + 1) * LANES]
            sq = a * a
            s0 = jnp.sum(jnp.where(lo, sq, 0.0), axis=-1, keepdims=True)
            s1 = jnp.sum(jnp.where(lo, 0.0, sq), axis=-1, keepdims=True)
            r0 = lax.rsqrt(s0 * (1.0 / HEAD_DIM) + EPS)
            r1 = lax.rsqrt(s1 * (1.0 / HEAD_DIM) + EPS)
            col = c * PROJ_CH + p * LANES
            o_ref[:, col:col + LANES] = (a * jnp.where(lo, r0, r1) * gain).astype(bf16)


def _proj_act_kernel(h_ref, w_ref, *refs, sigmoid, n_cast):
    cast_in, o_ref, cast_out = refs[:n_cast], refs[n_cast], refs[n_cast + 1:]
    h = h_ref[...]
    for c in range(PROJ_TN // PROJ_CH):
        cols = slice(c * PROJ_CH, (c + 1) * PROJ_CH)
        acc = jnp.dot(h, w_ref[:, cols], preferred_element_type=f32)
        if sigmoid:
            acc = 0.5 * jnp.tanh(0.5 * acc) + 0.5
        o_ref[:, cols] = acc.astype(bf16)
    for src, dst in zip(cast_in, cast_out):
        dst[...] = src[...].astype(bf16)


def _proj_qk(x2d, g_mix, w_in_bf, qk_gain):
    m = x2d.shape[0]
    return pl.pallas_call(
        _proj_qk_kernel,
        grid=(m // PROJ_TM, 2 * ATT_WIDTH // PROJ_TN),
        in_specs=[
            pl.BlockSpec((PROJ_TM, D_MODEL), lambda i, j: (i, 0)),
            pl.BlockSpec((1, D_MODEL), lambda i, j: (0, 0)),
            pl.BlockSpec((D_MODEL, PROJ_TN), lambda i, j: (0, j)),
            pl.BlockSpec((2, LANES), lambda i, j: (0, 0)),
        ],
        out_specs=[pl.BlockSpec((PROJ_TM, PROJ_TN), lambda i, j: (i, j)),
                   pl.BlockSpec((PROJ_TM, D_MODEL), lambda i, j: (i, 0))],
        out_shape=[jax.ShapeDtypeStruct((m, 2 * ATT_WIDTH), bf16),
                   jax.ShapeDtypeStruct((m, D_MODEL), bf16)],
        compiler_params=_cparams(("parallel", "arbitrary")),
        name="proj_qk",
    )(x2d, g_mix, w_in_bf, qk_gain)


def _proj_act(h, w_in_bf, col0, width, sigmoid, name, cast=()):
    m = h.shape[0]
    blk0 = col0 // PROJ_TN
    grid = (m // PROJ_TM, width // PROJ_TN)
    nsteps = grid[0] * grid[1]
    cast_specs = []
    for w in cast:
        rows = w.shape[0] // nsteps
        assert rows * nsteps == w.shape[0] and rows % CHUNK == 0
        cast_specs.append(pl.BlockSpec((rows, w.shape[1]), lambda i, j: (i * grid[1] + j, 0)))
    return pl.pallas_call(
        functools.partial(_proj_act_kernel, sigmoid=sigmoid, n_cast=len(cast)),
        grid=grid,
        in_specs=[
            pl.BlockSpec((PROJ_TM, D_MODEL), lambda i, j: (i, 0)),
            pl.BlockSpec((D_MODEL, PROJ_TN), lambda i, j: (0, j + blk0)),
        ] + cast_specs,
        out_specs=[pl.BlockSpec((PROJ_TM, PROJ_TN), lambda i, j: (i, j))] + cast_specs,
        out_shape=[jax.ShapeDtypeStruct((m, width), bf16)]
        + [jax.ShapeDtypeStruct(w.shape, bf16) for w in cast],
        compiler_params=_cparams(("parallel", "arbitrary")),
        name=name,
    )(h, w_in_bf, *cast)


ATT_QROWS = 8
ATT_TQ = ATT_QROWS * GRID_W
ATT_WROWS = ATT_QROWS + WIN_ROWS - 1
ATT_TK = ATT_WROWS * GRID_W
ATT_NKEY = WIN_ROWS * GRID_W
ATT_RPI = 2
ATT_UNITS = ATT_RPI * N_PAIRS


def _attn_window_start(b):
    return jnp.clip(ATT_QROWS * b - WIN_ROWS // 2, 0, ROWS - ATT_WROWS)


def _attn_kernel(q_ref, k_ref, v_ref, bias_ref, o_ref, s_scr):
    b = pl.program_id(1)
    ws = _attn_window_start(b)
    lane = lax.broadcasted_iota(jnp.int32, (1, LANES), 1)
    lo = lane < HEAD_DIM

    def rows_body(it, carry):
        units = []
        for sub in range(ATT_RPI):
            i = it * ATT_RPI + sub
            r = ATT_QROWS * b + i
            rstart = jnp.clip(r - WIN_ROWS // 2, 0, ROWS - WIN_ROWS)
            off = pl.multiple_of((rstart - ws) * GRID_W, GRID_W)
            ro_base = rstart - r + (WIN_ROWS - 1)
            qoff = pl.multiple_of(i * GRID_W, GRID_W)
            for p in range(N_PAIRS):
                units.append((p, slice(p * LANES, (p + 1) * LANES), off, ro_base, qoff))

        tiles = [slice(t * LANES, (t + 1) * LANES) for t in range(ATT_NKEY // LANES)]
        row_max = []
        for u, (p, cols, off, ro_base, qoff) in enumerate(units):
            qp = q_ref[pl.ds(qoff, GRID_W), cols]
            zero = jnp.zeros_like(qp)
            lhs = jnp.concatenate([jnp.where(lo, qp, zero), jnp.where(lo, zero, qp)], axis=0)
            kw = k_ref[pl.ds(off, ATT_NKEY), cols]
            s = lax.dot_general(lhs, kw, (((1,), (1,)), ((), ())),
                                preferred_element_type=f32)
            mx = None
            for t, tile in enumerate(tiles):
                st = s[:, tile] + bias_ref[p, ro_base + 2 * t]
                s_scr[u, :, tile] = st
                mx = st if mx is None else jnp.maximum(mx, st)
            row_max.append(jnp.max(mx, axis=-1, keepdims=True))

        for u, (p, cols, off, ro_base, qoff) in enumerate(units):
            acc = None
            o = None
            for t, tile in enumerate(tiles):
                e = jnp.exp(s_scr[u, :, tile] - row_max[u])
                acc = e if acc is None else acc + e
                vt = v_ref[pl.ds(off + t * LANES, LANES), cols]
                ot = jnp.dot(e.astype(bf16), vt, preferred_element_type=f32)
                o = ot if o is None else o + ot
            o = o / jnp.sum(acc, axis=-1, keepdims=True)
            o_ref[pl.ds(qoff, GRID_W), cols] = jnp.where(
                lo, o[:GRID_W], o[GRID_W:]).astype(bf16)
        return carry

    lax.fori_loop(0, ATT_QROWS // ATT_RPI, rows_body, 0)


def _attn_bias_table(rpb):
    c = np.arange(GRID_W)
    cstart = np.clip(c - WIN_COLS // 2, 0, GRID_W - WIN_COLS)
    kc = np.arange(GRID_W)
    valid = (kc[None, :] >= cstart[:, None]) & (kc[None, :] < cstart[:, None] + WIN_COLS)
    idx = kc[None, :] - c[:, None] + (WIN_COLS - 1)
    onehot = (valid[None] & (idx[None] == np.arange(2 * WIN_COLS - 1)[:, None, None]))
    t = jnp.einsum('hrj,jck->hrck', rpb.astype(f32), jnp.asarray(onehot, f32),
                   precision=lax.Precision.HIGHEST)
    t = t + jnp.asarray(np.where(valid, 0.0, NEG_BIAS), f32)
    t = t.reshape(N_PAIRS, 2, 2 * WIN_ROWS - 1, GRID_W, GRID_W)
    t = t.transpose(0, 2, 1, 3, 4).reshape(N_PAIRS, 2 * WIN_ROWS - 1, 2 * GRID_W, GRID_W)
    return jnp.concatenate([t[:, :-1], t[:, 1:]], axis=-1)


def _attn(zqk, zvf, bias_tab, batch):
    m = zqk.shape[0]
    nblk = ROWS // ATT_QROWS

    def kv_map(col):
        def index_map(s, b):
            return ((s * ROWS + _attn_window_start(b)) * GRID_W, col)
        return index_map

    return pl.pallas_call(
        _attn_kernel,
        grid=(batch, nblk),
        in_specs=[
            pl.BlockSpec((ATT_TQ, ATT_WIDTH), lambda s, b: (s * nblk + b, 0)),
            pl.BlockSpec((pl.Element(ATT_TK), pl.Element(ATT_WIDTH)), kv_map(ATT_WIDTH)),
            pl.BlockSpec((pl.Element(ATT_TK), pl.Element(ATT_WIDTH)), kv_map(0)),
            pl.BlockSpec(bias_tab.shape, lambda s, b: (0, 0, 0, 0)),
        ],
        out_specs=pl.BlockSpec((ATT_TQ, ATT_WIDTH), lambda s, b: (s * nblk + b, 0)),
        out_shape=jax.ShapeDtypeStruct((m, ATT_WIDTH), bf16),
        compiler_params=_cparams(("parallel", "arbitrary")),
        scratch_shapes=[pltpu.VMEM((ATT_UNITS, 2 * GRID_W, ATT_NKEY), f32)],
        name="attn",
    )(zqk, zqk, zvf, bias_tab)


def _dft_kernel(m1_ref, m2_ref, m3_ref, x_ref, o_ref, y_ref):
    for c in range(256 // CHUNK):
        rows = slice(c * CHUNK, (c + 1) * CHUNK)
        xin = x_ref[0, :, rows, :].reshape(R1 * CHUNK, FGROUP_DIM)
        res = jnp.dot(m1_ref[c], xin, preferred_element_type=f32)
        y_ref[:, :, c, :, :] = res.reshape(2, R1, CHUNK, FGROUP_DIM).astype(bf16)
    for k1 in range(R1):
        xin = y_ref[:, k1].reshape(2 * R2 * R3, FGROUP_DIM)
        res = jnp.dot(m2_ref[...], xin, preferred_element_type=f32)
        y_ref[:, k1] = res.reshape(2, R2, R3, FGROUP_DIM).astype(bf16)
    for h in range(R1 // CHUNK):
        k1s = slice(h * CHUNK, (h + 1) * CHUNK)
        for k2 in range(R2):
            xin = y_ref[:, k1s, k2, :, :].reshape(2 * CHUNK * R3, FGROUP_DIM)
            res = jnp.dot(m3_ref[...], xin, preferred_element_type=f32)
            o_ref[0, :, :, k2, k1s, :] = res.reshape(2, R3, CHUNK, FGROUP_DIM).astype(bf16)


def _dft(zvf, m1, m2, m3, batch):
    zv = zvf.reshape(batch, R1, 256, ATT_WIDTH + FOURIER_WIDTH)
    f_blk0 = ATT_WIDTH // FGROUP_DIM
    const = lambda a: pl.BlockSpec(a.shape, lambda s, g: (0,) * a.ndim,
                                   pipeline_mode=pl.Buffered(1))
    return pl.pallas_call(
        _dft_kernel,
        grid=(batch, N_FGROUPS),
        in_specs=[
            const(m1), const(m2), const(m3),
            pl.BlockSpec((1, R1, 256, FGROUP_DIM), lambda s, g: (s, 0, 0, f_blk0 + g)),
        ],
        out_specs=pl.BlockSpec((1, 2, R3, R2, R1, FGROUP_DIM), lambda s, g: (s, 0, 0, 0, 0, g)),
        out_shape=jax.ShapeDtypeStruct((batch, 2, R3, R2, R1, FOURIER_WIDTH), bf16),
        scratch_shapes=[pltpu.VMEM((2, R1, R2, R3, FGROUP_DIM), bf16)],
        compiler_params=_cparams(("parallel", "arbitrary")),
        name="dft",
    )(m1, m2, m3, zv)


MERGE_TM = 512


def _merge_kernel(p_ref, att_ref, sa_ref, sb_ref, x_ref, cc_ref, sc_ref,
                  wfo_ref, wao_ref, wout_ref, o_ref):
    fm = []
    for g in range(N_FGROUPS):
        cols = slice(g * FGROUP_DIM, (g + 1) * FGROUP_DIM)
        a = jnp.dot(p_ref[0, 0, :, cols], cc_ref[...], preferred_element_type=f32)
        a = a + jnp.dot(p_ref[0, 1, :, cols], sc_ref[...], preferred_element_type=f32)
        fm.append(a.astype(bf16))
    fm = jnp.concatenate(fm, axis=1)
    ya = jnp.dot(fm, wfo_ref[...], preferred_element_type=f32)
    yb = jnp.dot(att_ref[...], wao_ref[...], preferred_element_type=f32)
    merged = sa_ref[...].astype(f32) * ya + sb_ref[...].astype(f32) * yb
    o_ref[...] = x_ref[...] + jnp.dot(merged.astype(bf16), wout_ref[...],
                                      preferred_element_type=f32)


def _merge(pz, att, zg, x2d, cc, sc, wfo, wao, wout, batch):
    m = x2d.shape[0]
    per_seq = SEQ // MERGE_TM
    pv = pz.reshape(batch, 2, SEQ, FOURIER_WIDTH)
    const = lambda shape: pl.BlockSpec(shape, lambda i: (0, 0), pipeline_mode=pl.Buffered(1))
    return pl.pallas_call(
        _merge_kernel,
        grid=(m // MERGE_TM,),
        in_specs=[
            pl.BlockSpec((1, 2, MERGE_TM, FOURIER_WIDTH),
                         lambda i: (i // per_seq, 0, i % per_seq, 0)),
            pl.BlockSpec((MERGE_TM, ATT_WIDTH), lambda i: (i, 0)),
            pl.BlockSpec((MERGE_TM, D_MODEL), lambda i: (i, 0)),
            pl.BlockSpec((MERGE_TM, D_MODEL), lambda i: (i, 1)),
            pl.BlockSpec((MERGE_TM, D_MODEL), lambda i: (i, 0)),
            const(cc.shape), const(sc.shape), const(wfo.shape), const(wao.shape), const(wout.shape),
        ],
        out_specs=pl.BlockSpec((MERGE_TM, D_MODEL), lambda i: (i, 0)),
        out_shape=jax.ShapeDtypeStruct((m, D_MODEL), f32),
        compiler_params=_cparams(("parallel",)),
        name="merge",
    )(pv, att, zg, zg, x2d, cc, sc, wfo, wao, wout)


MLP_TM = 512
MLP_TF = 1024
MLP_CH = 256


def _mlp_kernel(x_ref, g_ref, w1_ref, w2_ref, o_ref, h_ref):
    j = pl.program_id(1)

    @pl.when(j == 0)
    def _():
        x = x_ref[...]
        h_ref[...] = _rmsnorm_rows(x, g_ref[...]).astype(bf16)
        o_ref[...] = x

    h = h_ref[...]
    us = []
    for c in range(MLP_TF // MLP_CH):
        u = jnp.dot(h, w1_ref[:, c * MLP_CH:(c + 1) * MLP_CH], preferred_element_type=f32)
        u = jnp.maximum(u, 0.0)
        us.append((u * u).astype(bf16))
    o_ref[...] += jnp.dot(jnp.concatenate(us, axis=1), w2_ref[...], preferred_element_type=f32)


def _mlp(x1, g_mlp, w1_bf, w2_bf):
    m = x1.shape[0]
    return pl.pallas_call(
        _mlp_kernel,
        grid=(m // MLP_TM, D_FF // MLP_TF),
        in_specs=[
            pl.BlockSpec((MLP_TM, D_MODEL), lambda i, j: (i, 0)),
            pl.BlockSpec((1, D_MODEL), lambda i, j: (0, 0)),
            pl.BlockSpec((D_MODEL, MLP_TF), lambda i, j: (0, j)),
            pl.BlockSpec((MLP_TF, D_MODEL), lambda i, j: (j, 0)),
        ],
        out_specs=pl.BlockSpec((MLP_TM, D_MODEL), lambda i, j: (i, 0)),
        out_shape=jax.ShapeDtypeStruct((m, D_MODEL), f32),
        scratch_shapes=[pltpu.VMEM((MLP_TM, D_MODEL), bf16)],
        compiler_params=_cparams(("parallel", "arbitrary")),
        name="mlp",
    )(x1, g_mlp, w1_bf, w2_bf)


def _encoder_block(x, consts, params, late_weights):
    batch, seq, _ = x.shape
    assert seq == SEQ
    m1, m2, m3, cc, sc = consts
    g_mix, w_in, qk_gain, bias_tab, g_mlp = params
    to_cast = late_weights[0].dtype != bf16
    x2d = x.reshape(batch * seq, D_MODEL)
    zqk, h = _proj_qk(x2d, g_mix, w_in, qk_gain)
    zvf, *mix_w = _proj_act(h, w_in, 2 * ATT_WIDTH, ATT_WIDTH + FOURIER_WIDTH, False, "proj_vf",
                            cast=late_weights[:3] if to_cast else ())
    zg, *mlp_w = _proj_act(h, w_in, 3 * ATT_WIDTH + FOURIER_WIDTH, 2 * D_MODEL, True, "proj_gate",
                           cast=late_weights[3:] if to_cast else ())
    if to_cast:
        late_weights = tuple(mix_w) + tuple(mlp_w)
    w_fo, w_ao, w_out, w1, w2 = late_weights
    att = _attn(zqk, zvf, bias_tab, batch)
    pz = _dft(zvf, m1, m2, m3, batch)
    x1 = _merge(pz, att, zg, x2d, cc, sc, w_fo, w_ao, w_out, batch)
    out = _mlp(x1, g_mlp, w1, w2)
    return out.reshape(batch, seq, D_MODEL), late_weights


def kernel(x_prompt, x_sample, g_mix, w_in, q_norm, k_norm, rpb, w_fo, w_ao, w_out, g_mlp, w1, w2):
    cc_np, sc_np = _channel_matrices()
    consts = tuple(
        jnp.asarray(a, f32).astype(bf16)
        for a in (_stage1_matrices(), _stage2_matrix(), _stage3_matrix(), cc_np, sc_np))
    y_prompt, y_sample = x_prompt, x_sample
    for l in range(g_mix.shape[0]):
        qk_gain = jnp.stack([
            jnp.tile(q_norm[l].astype(f32), 2) * (HEAD_DIM ** -0.5),
            jnp.tile(k_norm[l].astype(f32), 2),
        ])
        params = (
            g_mix[l].astype(f32).reshape(1, D_MODEL), w_in[l].astype(bf16), qk_gain,
            _attn_bias_table(rpb[l]), g_mlp[l].astype(f32).reshape(1, D_MODEL),
        )
        late_weights = tuple(w[l].astype(f32) for w in (w_fo, w_ao, w_out, w1, w2))
        y_prompt, late_weights = _encoder_block(y_prompt, consts, params, late_weights)
        y_sample, late_weights = _encoder_block(y_sample, consts, params, late_weights)
    return (y_prompt, y_sample)
```

```python
import functools

import numpy as np
import jax
import jax.numpy as jnp
from jax import lax
from jax.experimental import pallas as pl
from jax.experimental.pallas import tpu as pltpu

D_MODEL = 2048
GRID_W = 64
N_HEADS = 16
HEAD_DIM = 64
ATT_WIDTH = N_HEADS * HEAD_DIM
N_FGROUPS = 4
FGROUP_DIM = 256
FOURIER_WIDTH = N_FGROUPS * FGROUP_DIM
WIN_ROWS = 8
WIN_COLS = 16
D_FF = 4 * D_MODEL
IN_WIDTH = 3 * ATT_WIDTH + FOURIER_WIDTH + 2 * D_MODEL
EPS = 1e-6

SEQ = 8192
ROWS = SEQ // GRID_W
LANES = 128
N_PAIRS = N_HEADS // 2
NEG_BIAS = -1e30

R1, R2, R3 = 32, 16, 16
CHUNK = 16
HALF = 8

VMEM_LIMIT = 56 * 1024 * 1024

bf16 = jnp.bfloat16
f32 = jnp.float32


def _cparams(sem):
    return pltpu.CompilerParams(dimension_semantics=sem, vmem_limit_bytes=VMEM_LIMIT)


def _stage1_matrices():
    mc = np.arange(256 // HALF)[:, None, None, None]
    k1 = np.arange(R1)[None, :, None, None]
    ml = np.arange(HALF)[None, None, :, None]
    n1 = np.arange(R1)[None, None, None, :]
    theta = 2.0 * np.pi * (((256 * n1 + HALF * mc + ml) * k1) % SEQ) / SEQ
    eye = np.eye(HALF)
    scale = 1.0 / np.sqrt(R1)
    out = np.zeros((256 // HALF, 2, R1, HALF, R1, HALF), np.float64)
    for part, fn in enumerate((np.cos, lambda t: -np.sin(t))):
        vals = fn(theta) * scale
        out[:, part] = vals[:, :, :, :, None] * eye[None, None, :, None, :]
    return out.reshape(256 // HALF, 2 * R1 * HALF, R1 * HALF)


def _complex_block(cos, sin):
    return np.block([[cos, sin], [-sin, cos]])


def _stage2_matrices():
    mats = []
    for hf in range(R3 // HALF):
        k2 = np.arange(R2)[:, None, None, None]
        n3o = np.arange(HALF)[None, :, None, None]
        n2 = np.arange(R2)[None, None, :, None]
        n3 = np.arange(HALF)[None, None, None, :]
        phi = 2.0 * np.pi * (((16 * n2 + HALF * hf + n3) * k2) % 256) / 256
        delta = (n3o == n3).astype(np.float64)
        scale = 1.0 / np.sqrt(R2)
        cos = (np.cos(phi) * delta * scale).reshape(R2 * HALF, R2 * HALF)
        sin = (np.sin(phi) * delta * scale).reshape(R2 * HALF, R2 * HALF)
        mats.append(_complex_block(cos, sin))
    return np.stack(mats)


def _stage3_matrix():
    k3 = np.arange(R3)[:, None, None, None]
    k1o = np.arange(CHUNK)[None, :, None, None]
    k1 = np.arange(CHUNK)[None, None, :, None]
    n3 = np.arange(R3)[None, None, None, :]
    psi = 2.0 * np.pi * ((n3 * k3) % R3) / R3
    delta = (k1o == k1).astype(np.float64)
    scale = 1.0 / np.sqrt(R3)
    cos = (np.cos(psi) * delta * scale).reshape(R3 * CHUNK, CHUNK * R3)
    sin = (np.sin(psi) * delta * scale).reshape(R3 * CHUNK, CHUNK * R3)
    return _complex_block(cos, sin)


def _channel_matrices():
    c = np.arange(FGROUP_DIM)
    ang = 2.0 * np.pi * ((c[:, None] * c[None, :]) % FGROUP_DIM) / FGROUP_DIM
    scale = 1.0 / np.sqrt(FGROUP_DIM)
    return np.cos(ang) * scale, np.sin(ang) * scale


PROJ_TM = 1024
PROJ_TN = 1024


def _rmsnorm_rows(x, g):
    ms = jnp.mean(x * x, axis=-1, keepdims=True)
    return x * lax.rsqrt(ms + EPS) * g


PROJ_CH = 256


def _proj_qk_kernel(x_ref, g_ref, w_ref, gain_ref, o_ref, h_ref):
    j = pl.program_id(1)

    @pl.when(j == 0)
    def _():
        h_ref[...] = _rmsnorm_rows(x_ref[...], g_ref[...]).astype(bf16)

    gain = gain_ref[pl.ds(j, 1), :]
    lane = lax.broadcasted_iota(jnp.int32, (1, LANES), 1)
    lo = lane < HEAD_DIM
    h = h_ref[...]
    for c in range(PROJ_TN // PROJ_CH):
        acc = jnp.dot(h, w_ref[:, c * PROJ_CH:(c + 1) * PROJ_CH], preferred_element_type=f32)
        for p in range(PROJ_CH // LANES):
            a = acc[:, p * LANES:(p + 1) * LANES]
            sq = a * a
            s0 = jnp.sum(jnp.where(lo, sq, 0.0), axis=-1, keepdims=True)
            s1 = jnp.sum(jnp.where(lo, 0.0, sq), axis=-1, keepdims=True)
            r0 = lax.rsqrt(s0 * (1.0 / HEAD_DIM) + EPS)
            r1 = lax.rsqrt(s1 * (1.0 / HEAD_DIM) + EPS)
            col = c * PROJ_CH + p * LANES
            o_ref[:, col:col + LANES] = (a * jnp.where(lo, r0, r1) * gain).astype(bf16)


def _proj_act_kernel(h_ref, w_ref, *refs, sigmoid, n_cast):
    cast_in, o_ref, cast_out = refs[:n_cast], refs[n_cast], refs[n_cast + 1:]
    h = h_ref[...]
    for c in range(PROJ_TN // PROJ_CH):
        cols = slice(c * PROJ_CH, (c + 1) * PROJ_CH)
        acc = jnp.dot(h, w_ref[:, cols], preferred_element_type=f32)
        if sigmoid:
            acc = 0.5 * jnp.tanh(0.5 * acc) + 0.5
        o_ref[:, cols] = acc.astype(bf16)
    for src, dst in zip(cast_in, cast_out):
        dst[...] = src[...].astype(bf16)


def _proj_qk(x2d, g_mix, w_in_bf, qk_gain):
    m = x2d.shape[0]
    return pl.pallas_call(
        _proj_qk_kernel,
        grid=(m // PROJ_TM, 2 * ATT_WIDTH // PROJ_TN),
        in_specs=[
            pl.BlockSpec((PROJ_TM, D_MODEL), lambda i, j: (i, 0)),
            pl.BlockSpec((1, D_MODEL), lambda i, j: (0, 0)),
            pl.BlockSpec((D_MODEL, PROJ_TN), lambda i, j: (0, j)),
            pl.BlockSpec((2, LANES), lambda i, j: (0, 0)),
        ],
        out_specs=[pl.BlockSpec((PROJ_TM, PROJ_TN), lambda i, j: (i, j)),
                   pl.BlockSpec((PROJ_TM, D_MODEL), lambda i, j: (i, 0))],
        out_shape=[jax.ShapeDtypeStruct((m, 2 * ATT_WIDTH), bf16),
                   jax.ShapeDtypeStruct((m, D_MODEL), bf16)],
        compiler_params=_cparams(("parallel", "arbitrary")),
        name="proj_qk",
    )(x2d, g_mix, w_in_bf, qk_gain)


def _proj_act(h, w_in_bf, col0, width, sigmoid, name, cast=()):
    m = h.shape[0]
    blk0 = col0 // PROJ_TN
    grid = (m // PROJ_TM, width // PROJ_TN)
    nsteps = grid[0] * grid[1]
    cast_specs = []
    for w in cast:
        rows = w.shape[0] // nsteps
        assert rows * nsteps == w.shape[0] and rows % CHUNK == 0
        cast_specs.append(pl.BlockSpec((rows, w.shape[1]), lambda i, j: (i * grid[1] + j, 0)))
    return pl.pallas_call(
        functools.partial(_proj_act_kernel, sigmoid=sigmoid, n_cast=len(cast)),
        grid=grid,
        in_specs=[
            pl.BlockSpec((PROJ_TM, D_MODEL), lambda i, j: (i, 0)),
            pl.BlockSpec((D_MODEL, PROJ_TN), lambda i, j: (0, j + blk0)),
        ] + cast_specs,
        out_specs=[pl.BlockSpec((PROJ_TM, PROJ_TN), lambda i, j: (i, j))] + cast_specs,
        out_shape=[jax.ShapeDtypeStruct((m, width), bf16)]
        + [jax.ShapeDtypeStruct(w.shape, bf16) for w in cast],
        compiler_params=_cparams(("parallel", "arbitrary")),
        name=name,
    )(h, w_in_bf, *cast)


ATT_QROWS = 8
ATT_TQ = ATT_QROWS * GRID_W
ATT_WROWS = ATT_QROWS + WIN_ROWS - 1
ATT_TK = ATT_WROWS * GRID_W
ATT_NKEY = WIN_ROWS * GRID_W
ATT_RPI = 2
ATT_UNITS = ATT_RPI * N_PAIRS


def _attn_window_start(b):
    return jnp.clip(ATT_QROWS * b - WIN_ROWS // 2, 0, ROWS - ATT_WROWS)


def _attn_kernel(q_ref, k_ref, v_ref, bias_ref, o_ref, s_scr):
    b = pl.program_id(1)
    ws = _attn_window_start(b)
    lane = lax.broadcasted_iota(jnp.int32, (1, LANES), 1)
    lo = lane < HEAD_DIM

    def rows_body(it, carry):
        units = []
        for sub in range(ATT_RPI):
            i = it * ATT_RPI + sub
            r = ATT_QROWS * b + i
            rstart = jnp.clip(r - WIN_ROWS // 2, 0, ROWS - WIN_ROWS)
            off = pl.multiple_of((rstart - ws) * GRID_W, GRID_W)
            ro_base = rstart - r + (WIN_ROWS - 1)
            qoff = pl.multiple_of(i * GRID_W, GRID_W)
            for p in range(N_PAIRS):
                units.append((p, slice(p * LANES, (p + 1) * LANES), off, ro_base, qoff))

        tiles = [slice(t * LANES, (t + 1) * LANES) for t in range(ATT_NKEY // LANES)]
        row_max = []
        for u, (p, cols, off, ro_base, qoff) in enumerate(units):
            qp = q_ref[pl.ds(qoff, GRID_W), cols]
            zero = jnp.zeros_like(qp)
            lhs = jnp.concatenate([jnp.where(lo, qp, zero), jnp.where(lo, zero, qp)], axis=0)
            kw = k_ref[pl.ds(off, ATT_NKEY), cols]
            s = lax.dot_general(lhs, kw, (((1,), (1,)), ((), ())),
                                preferred_element_type=f32)
            mx = None
            for t, tile in enumerate(tiles):
                st = s[:, tile] + bias_ref[p, ro_base + 2 * t]
                s_scr[u, :, tile] = st
                mx = st if mx is None else jnp.maximum(mx, st)
            row_max.append(jnp.max(mx, axis=-1, keepdims=True))

        for u, (p, cols, off, ro_base, qoff) in enumerate(units):
            acc = None
            o = None
            for t, tile in enumerate(tiles):
                e = jnp.exp(s_scr[u, :, tile] - row_max[u])
                acc = e if acc is None else acc + e
                vt = v_ref[pl.ds(off + t * LANES, LANES), cols]
                ot = jnp.dot(e.astype(bf16), vt, preferred_element_type=f32)
                o = ot if o is None else o + ot
            o = o / jnp.sum(acc, axis=-1, keepdims=True)
            o_ref[pl.ds(qoff, GRID_W), cols] = jnp.where(
                lo, o[:GRID_W], o[GRID_W:]).astype(bf16)
        return carry

    lax.fori_loop(0, ATT_QROWS // ATT_RPI, rows_body, 0)


def _attn_bias_table(rpb):
    c = np.arange(GRID_W)
    cstart = np.clip(c - WIN_COLS // 2, 0, GRID_W - WIN_COLS)
    kc = np.arange(GRID_W)
    valid = (kc[None, :] >= cstart[:, None]) & (kc[None, :] < cstart[:, None] + WIN_COLS)
    idx = kc[None, :] - c[:, None] + (WIN_COLS - 1)
    onehot = (valid[None] & (idx[None] == np.arange(2 * WIN_COLS - 1)[:, None, None]))
    t = jnp.einsum('hrj,jck->hrck', rpb.astype(f32), jnp.asarray(onehot, f32),
                   precision=lax.Precision.HIGHEST)
    t = t + jnp.asarray(np.where(valid, 0.0, NEG_BIAS), f32)
    t = t.reshape(N_PAIRS, 2, 2 * WIN_ROWS - 1, GRID_W, GRID_W)
    t = t.transpose(0, 2, 1, 3, 4).reshape(N_PAIRS, 2 * WIN_ROWS - 1, 2 * GRID_W, GRID_W)
    return jnp.concatenate([t[:, :-1], t[:, 1:]], axis=-1)


def _attn(zqk, zvf, bias_tab, batch):
    m = zqk.shape[0]
    nblk = ROWS // ATT_QROWS

    def kv_map(col):
        def index_map(s, b):
            return ((s * ROWS + _attn_window_start(b)) * GRID_W, col)
        return index_map

    return pl.pallas_call(
        _attn_kernel,
        grid=(batch, nblk),
        in_specs=[
            pl.BlockSpec((ATT_TQ, ATT_WIDTH), lambda s, b: (s * nblk + b, 0)),
            pl.BlockSpec((pl.Element(ATT_TK), pl.Element(ATT_WIDTH)), kv_map(ATT_WIDTH)),
            pl.BlockSpec((pl.Element(ATT_TK), pl.Element(ATT_WIDTH)), kv_map(0)),
            pl.BlockSpec(bias_tab.shape, lambda s, b: (0, 0, 0, 0)),
        ],
        out_specs=pl.BlockSpec((ATT_TQ, ATT_WIDTH), lambda s, b: (s * nblk + b, 0)),
        out_shape=jax.ShapeDtypeStruct((m, ATT_WIDTH), bf16),
        compiler_params=_cparams(("parallel", "arbitrary")),
        scratch_shapes=[pltpu.VMEM((ATT_UNITS, 2 * GRID_W, ATT_NKEY), f32)],
        name="attn",
    )(zqk, zqk, zvf, bias_tab)


def _dft_kernel(m1_ref, m2_ref, m3_ref, x_ref, o_ref, y_ref):
    for c in range(256 // CHUNK):
        x16 = x_ref[0, :, c * CHUNK:(c + 1) * CHUNK, :].astype(f32)
        for hf in range(CHUNK // HALF):
            rows = slice(hf * HALF, (hf + 1) * HALF)
            xin = x16[:, rows, :].reshape(R1 * HALF, FGROUP_DIM).astype(bf16)
            res = jnp.dot(m1_ref[2 * c + hf], xin, preferred_element_type=f32)
            y_ref[:, :, c, rows, :] = res.reshape(2, R1, HALF, FGROUP_DIM)
    for k1 in range(R1):
        for hf in range(R3 // HALF):
            rows = slice(hf * HALF, (hf + 1) * HALF)
            xin = y_ref[:, k1, :, rows, :].reshape(2 * R2 * HALF, FGROUP_DIM).astype(bf16)
            res = jnp.dot(m2_ref[hf], xin, preferred_element_type=f32)
            y_ref[:, k1, :, rows, :] = res.reshape(2, R2, HALF, FGROUP_DIM)
    for h in range(R1 // CHUNK):
        k1s = slice(h * CHUNK, (h + 1) * CHUNK)
        for k2 in range(R2):
            xin = y_ref[:, k1s, k2, :, :].reshape(2 * CHUNK * R3, FGROUP_DIM).astype(bf16)
            res = jnp.dot(m3_ref[...], xin, preferred_element_type=f32)
            o_ref[0, :, :, k2, k1s, :] = res.reshape(2, R3, CHUNK, FGROUP_DIM).astype(bf16)


def _dft(zvf, m1, m2, m3, batch):
    zv = zvf.reshape(batch, R1, 256, ATT_WIDTH + FOURIER_WIDTH)
    f_blk0 = ATT_WIDTH // FGROUP_DIM
    const = lambda a: pl.BlockSpec(a.shape, lambda s, g: (0,) * a.ndim,
                                   pipeline_mode=pl.Buffered(1))
    return pl.pallas_call(
        _dft_kernel,
        grid=(batch, N_FGROUPS),
        in_specs=[
            const(m1), const(m2), const(m3),
            pl.BlockSpec((1, R1, 256, FGROUP_DIM), lambda s, g: (s, 0, 0, f_blk0 + g)),
        ],
        out_specs=pl.BlockSpec((1, 2, R3, R2, R1, FGROUP_DIM), lambda s, g: (s, 0, 0, 0, 0, g)),
        out_shape=jax.ShapeDtypeStruct((batch, 2, R3, R2, R1, FOURIER_WIDTH), bf16),
        scratch_shapes=[pltpu.VMEM((2, R1, R2, R3, FGROUP_DIM), f32)],
        compiler_params=_cparams(("parallel", "arbitrary")),
        name="dft",
    )(m1, m2, m3, zv)


MERGE_TM = 512


def _merge_kernel(p_ref, att_ref, sa_ref, sb_ref, x_ref, cc_ref, sc_ref,
                  wfo_ref, wao_ref, wout_ref, o_ref):
    fm = []
    for g in range(N_FGROUPS):
        cols = slice(g * FGROUP_DIM, (g + 1) * FGROUP_DIM)
        a = jnp.dot(p_ref[0, 0, :, cols], cc_ref[...], preferred_element_type=f32)
        a = a + jnp.dot(p_ref[0, 1, :, cols], sc_ref[...], preferred_element_type=f32)
        fm.append(a.astype(bf16))
    fm = jnp.concatenate(fm, axis=1)
    ya = jnp.dot(fm, wfo_ref[...], preferred_element_type=f32)
    yb = jnp.dot(att_ref[...], wao_ref[...], preferred_element_type=f32)
    merged = sa_ref[...].astype(f32) * ya + sb_ref[...].astype(f32) * yb
    o_ref[...] = x_ref[...] + jnp.dot(merged.astype(bf16), wout_ref[...],
                                      preferred_element_type=f32)


def _merge(pz, att, zg, x2d, cc, sc, wfo, wao, wout, batch):
    m = x2d.shape[0]
    per_seq = SEQ // MERGE_TM
    pv = pz.reshape(batch, 2, SEQ, FOURIER_WIDTH)
    const = lambda shape: pl.BlockSpec(shape, lambda i: (0, 0), pipeline_mode=pl.Buffered(1))
    return pl.pallas_call(
        _merge_kernel,
        grid=(m // MERGE_TM,),
        in_specs=[
            pl.BlockSpec((1, 2, MERGE_TM, FOURIER_WIDTH),
                         lambda i: (i // per_seq, 0, i % per_seq, 0)),
            pl.BlockSpec((MERGE_TM, ATT_WIDTH), lambda i: (i, 0)),
            pl.BlockSpec((MERGE_TM, D_MODEL), lambda i: (i, 0)),
            pl.BlockSpec((MERGE_TM, D_MODEL), lambda i: (i, 1)),
            pl.BlockSpec((MERGE_TM, D_MODEL), lambda i: (i, 0)),
            const(cc.shape), const(sc.shape), const(wfo.shape), const(wao.shape), const(wout.shape),
        ],
        out_specs=pl.BlockSpec((MERGE_TM, D_MODEL), lambda i: (i, 0)),
        out_shape=jax.ShapeDtypeStruct((m, D_MODEL), f32),
        compiler_params=_cparams(("parallel",)),
        name="merge",
    )(pv, att, zg, zg, x2d, cc, sc, wfo, wao, wout)


MLP_TM = 512
MLP_TF = 1024
MLP_CH = 256


def _mlp_kernel(x_ref, g_ref, w1_ref, w2_ref, o_ref, h_ref):
    j = pl.program_id(1)

    @pl.when(j == 0)
    def _():
        x = x_ref[...]
        h_ref[...] = _rmsnorm_rows(x, g_ref[...]).astype(bf16)
        o_ref[...] = x

    h = h_ref[...]
    us = []
    for c in range(MLP_TF // MLP_CH):
        u = jnp.dot(h, w1_ref[:, c * MLP_CH:(c + 1) * MLP_CH], preferred_element_type=f32)
        u = jnp.maximum(u, 0.0)
        us.append((u * u).astype(bf16))
    o_ref[...] += jnp.dot(jnp.concatenate(us, axis=1), w2_ref[...], preferred_element_type=f32)


def _mlp(x1, g_mlp, w1_bf, w2_bf):
    m = x1.shape[0]
    return pl.pallas_call(
        _mlp_kernel,
        grid=(m // MLP_TM, D_FF // MLP_TF),
        in_specs=[
            pl.BlockSpec((MLP_TM, D_MODEL), lambda i, j: (i, 0)),
            pl.BlockSpec((1, D_MODEL), lambda i, j: (0, 0)),
            pl.BlockSpec((D_MODEL, MLP_TF), lambda i, j: (0, j)),
            pl.BlockSpec((MLP_TF, D_MODEL), lambda i, j: (j, 0)),
        ],
        out_specs=pl.BlockSpec((MLP_TM, D_MODEL), lambda i, j: (i, 0)),
        out_shape=jax.ShapeDtypeStruct((m, D_MODEL), f32),
        scratch_shapes=[pltpu.VMEM((MLP_TM, D_MODEL), bf16)],
        compiler_params=_cparams(("parallel", "arbitrary")),
        name="mlp",
    )(x1, g_mlp, w1_bf, w2_bf)


def _encoder_block(x, consts, params, late_weights):
    batch, seq, _ = x.shape
    assert seq == SEQ
    m2, m3, cc, sc = consts
    g_mix, w_in, qk_gain, bias_tab, g_mlp = params
    to_cast = late_weights[0].dtype != bf16
    x2d = x.reshape(batch * seq, D_MODEL)
    zqk, h = _proj_qk(x2d, g_mix, w_in, qk_gain)
    zvf, *mix_w = _proj_act(h, w_in, 2 * ATT_WIDTH, ATT_WIDTH + FOURIER_WIDTH, False, "proj_vf",
                            cast=late_weights[:4] if to_cast else ())
    zg, *mlp_w = _proj_act(h, w_in, 3 * ATT_WIDTH + FOURIER_WIDTH, 2 * D_MODEL, True, "proj_gate",
                           cast=late_weights[4:] if to_cast else ())
    if to_cast:
        late_weights = tuple(mix_w) + tuple(mlp_w)
    w_fo, w_ao, w_out, m1, w1, w2 = late_weights
    att = _attn(zqk, zvf, bias_tab, batch)
    pz = _dft(zvf, m1.reshape(256 // HALF, 2 * R1 * HALF, R1 * HALF), m2, m3, batch)
    x1 = _merge(pz, att, zg, x2d, cc, sc, w_fo, w_ao, w_out, batch)
    out = _mlp(x1, g_mlp, w1, w2)
    return out.reshape(batch, seq, D_MODEL), late_weights


def kernel(x_prompt, x_sample, g_mix, w_in, q_norm, k_norm, rpb, w_fo, w_ao, w_out, g_mlp, w1, w2):
    cc_np, sc_np = _channel_matrices()
    consts = tuple(
        jnp.asarray(a, f32).astype(bf16)
        for a in (_stage2_matrices(), _stage3_matrix(), cc_np, sc_np))
    m1 = jnp.asarray(_stage1_matrices().reshape(-1, R1 * HALF), f32)
    y_prompt, y_sample = x_prompt, x_sample
    for l in range(g_mix.shape[0]):
        qk_gain = jnp.stack([
            jnp.tile(q_norm[l].astype(f32), 2) * (HEAD_DIM ** -0.5),
            jnp.tile(k_norm[l].astype(f32), 2),
        ])
        params = (
            g_mix[l].astype(f32).reshape(1, D_MODEL), w_in[l].astype(bf16), qk_gain,
            _attn_bias_table(rpb[l]), g_mlp[l].astype(f32).reshape(1, D_MODEL),
        )
        late_weights = (w_fo[l].astype(f32), w_ao[l].astype(f32), w_out[l].astype(f32), m1,
                        w1[l].astype(f32), w2[l].astype(f32))
        y_prompt, late_weights = _encoder_block(y_prompt, consts, params, late_weights)
        y_sample, late_weights = _encoder_block(y_sample, consts, params, late_weights)
    return (y_prompt, y_sample)
```

```python
import functools

import numpy as np
import jax
import jax.numpy as jnp
from jax import lax
from jax.experimental import pallas as pl
from jax.experimental.pallas import tpu as pltpu

D_MODEL = 2048
GRID_W = 64
N_HEADS = 16
HEAD_DIM = 64
ATT_WIDTH = N_HEADS * HEAD_DIM
N_FGROUPS = 4
FGROUP_DIM = 256
FOURIER_WIDTH = N_FGROUPS * FGROUP_DIM
WIN_ROWS = 8
WIN_COLS = 16
D_FF = 4 * D_MODEL
IN_WIDTH = 3 * ATT_WIDTH + FOURIER_WIDTH + 2 * D_MODEL
EPS = 1e-6

SEQ = 8192
ROWS = SEQ // GRID_W
LANES = 128
N_PAIRS = N_HEADS // 2
NEG_BIAS = -1e30

R1, R2, R3 = 32, 16, 16
CHUNK = 16
HALF = 8

VMEM_LIMIT = 56 * 1024 * 1024

bf16 = jnp.bfloat16
f32 = jnp.float32


def _cparams(sem):
    return pltpu.CompilerParams(dimension_semantics=sem, vmem_limit_bytes=VMEM_LIMIT)


def _stage1_matrices():
    mc = np.arange(256 // HALF)[:, None, None, None]
    k1 = np.arange(R1)[None, :, None, None]
    ml = np.arange(HALF)[None, None, :, None]
    n1 = np.arange(R1)[None, None, None, :]
    theta = 2.0 * np.pi * (((256 * n1 + HALF * mc + ml) * k1) % SEQ) / SEQ
    eye = np.eye(HALF)
    scale = 1.0 / np.sqrt(R1)
    out = np.zeros((256 // HALF, 2, R1, HALF, R1, HALF), np.float64)
    for part, fn in enumerate((np.cos, lambda t: -np.sin(t))):
        vals = fn(theta) * scale
        out[:, part] = vals[:, :, :, :, None] * eye[None, None, :, None, :]
    return out.reshape(256 // HALF, 2 * R1 * HALF, R1 * HALF)


def _complex_block(cos, sin):
    return np.block([[cos, sin], [-sin, cos]])


def _stage2_matrices():
    mats = []
    for hf in range(R3 // HALF):
        k2 = np.arange(R2)[:, None, None, None]
        n3o = np.arange(HALF)[None, :, None, None]
        n2 = np.arange(R2)[None, None, :, None]
        n3 = np.arange(HALF)[None, None, None, :]
        phi = 2.0 * np.pi * (((16 * n2 + HALF * hf + n3) * k2) % 256) / 256
        delta = (n3o == n3).astype(np.float64)
        scale = 1.0 / np.sqrt(R2)
        cos = (np.cos(phi) * delta * scale).reshape(R2 * HALF, R2 * HALF)
        sin = (np.sin(phi) * delta * scale).reshape(R2 * HALF, R2 * HALF)
        mats.append(_complex_block(cos, sin))
    return np.stack(mats)


def _stage3_matrix():
    k3 = np.arange(R3)[:, None, None, None]
    k1o = np.arange(CHUNK)[None, :, None, None]
    k1 = np.arange(CHUNK)[None, None, :, None]
    n3 = np.arange(R3)[None, None, None, :]
    psi = 2.0 * np.pi * ((n3 * k3) % R3) / R3
    delta = (k1o == k1).astype(np.float64)
    scale = 1.0 / np.sqrt(R3)
    cos = (np.cos(psi) * delta * scale).reshape(R3 * CHUNK, CHUNK * R3)
    sin = (np.sin(psi) * delta * scale).reshape(R3 * CHUNK, CHUNK * R3)
    return _complex_block(cos, sin)


def _channel_matrices():
    c = np.arange(FGROUP_DIM)
    ang = 2.0 * np.pi * ((c[:, None] * c[None, :]) % FGROUP_DIM) / FGROUP_DIM
    scale = 1.0 / np.sqrt(FGROUP_DIM)
    return np.cos(ang) * scale, np.sin(ang) * scale


PROJ_TM = 1024
PROJ_TN = 1024


def _rmsnorm_rows(x, g):
    ms = jnp.mean(x * x, axis=-1, keepdims=True)
    return x * lax.rsqrt(ms + EPS) * g


PROJ_CH = 256


def _proj_qk_kernel(x_ref, g_ref, w_ref, gain_ref, *refs, n_cast):
    cast_in, (o_ref, h_ref), cast_out = refs[:n_cast], refs[n_cast:n_cast + 2], refs[n_cast + 2:]
    for src, dst in zip(cast_in, cast_out):
        dst[...] = src[...].astype(bf16)
    j = pl.program_id(1)

    @pl.when(j == 0)
    def _():
        h_ref[...] = _rmsnorm_rows(x_ref[...], g_ref[...]).astype(bf16)

    gain = gain_ref[pl.ds(j, 1), :]
    lane = lax.broadcasted_iota(jnp.int32, (1, LANES), 1)
    lo = lane < HEAD_DIM
    h = h_ref[...]
    for c in range(PROJ_TN // PROJ_CH):
        acc = jnp.dot(h, w_ref[:, c * PROJ_CH:(c + 1) * PROJ_CH], preferred_element_type=f32)
        for p in range(PROJ_CH // LANES):
            a = acc[:, p * LANES:(p + 1) * LANES]
            sq = a * a
            s0 = jnp.sum(jnp.where(lo, sq, 0.0), axis=-1, keepdims=True)
            s1 = jnp.sum(jnp.where(lo, 0.0, sq), axis=-1, keepdims=True)
            r0 = lax.rsqrt(s0 * (1.0 / HEAD_DIM) + EPS)
            r1 = lax.rsqrt(s1 * (1.0 / HEAD_DIM) + EPS)
            col = c * PROJ_CH + p * LANES
            o_ref[:, col:col + LANES] = (a * jnp.where(lo, r0, r1) * gain).astype(bf16)


def _proj_act_kernel(h_ref, w_ref, *refs, sigmoid, n_cast):
    cast_in, o_ref, cast_out = refs[:n_cast], refs[n_cast], refs[n_cast + 1:]
    h = h_ref[...]
    for c in range(PROJ_TN // PROJ_CH):
        cols = slice(c * PROJ_CH, (c + 1) * PROJ_CH)
        acc = jnp.dot(h, w_ref[:, cols], preferred_element_type=f32)
        if sigmoid:
            acc = 0.5 * jnp.tanh(0.5 * acc) + 0.5
        o_ref[:, cols] = acc.astype(bf16)
    for src, dst in zip(cast_in, cast_out):
        dst[...] = src[...].astype(bf16)


def _cast_specs(cast, grid):
    nsteps = grid[0] * grid[1]
    specs = []
    for w in cast:
        rows = w.shape[0] // nsteps
        assert rows * nsteps == w.shape[0] and rows % CHUNK == 0
        specs.append(pl.BlockSpec((rows, w.shape[1]), lambda i, j: (i * grid[1] + j, 0)))
    return specs


def _proj_qk(x2d, g_mix, w_qk_bf, qk_gain, cast=()):
    m = x2d.shape[0]
    grid = (m // PROJ_TM, 2 * ATT_WIDTH // PROJ_TN)
    cast_specs = _cast_specs(cast, grid)
    return pl.pallas_call(
        functools.partial(_proj_qk_kernel, n_cast=len(cast)),
        grid=grid,
        in_specs=[
            pl.BlockSpec((PROJ_TM, D_MODEL), lambda i, j: (i, 0)),
            pl.BlockSpec((1, D_MODEL), lambda i, j: (0, 0)),
            pl.BlockSpec((D_MODEL, PROJ_TN), lambda i, j: (0, j)),
            pl.BlockSpec((2, LANES), lambda i, j: (0, 0)),
        ] + cast_specs,
        out_specs=[pl.BlockSpec((PROJ_TM, PROJ_TN), lambda i, j: (i, j)),
                   pl.BlockSpec((PROJ_TM, D_MODEL), lambda i, j: (i, 0))] + cast_specs,
        out_shape=[jax.ShapeDtypeStruct((m, 2 * ATT_WIDTH), bf16),
                   jax.ShapeDtypeStruct((m, D_MODEL), bf16)]
        + [jax.ShapeDtypeStruct(w.shape, bf16) for w in cast],
        compiler_params=_cparams(("parallel", "arbitrary")),
        name="proj_qk",
    )(x2d, g_mix, w_qk_bf, qk_gain, *cast)


def _proj_act(h, w_in_bf, col0, width, sigmoid, name, cast=()):
    m = h.shape[0]
    blk0 = col0 // PROJ_TN
    grid = (m // PROJ_TM, width // PROJ_TN)
    cast_specs = _cast_specs(cast, grid)
    return pl.pallas_call(
        functools.partial(_proj_act_kernel, sigmoid=sigmoid, n_cast=len(cast)),
        grid=grid,
        in_specs=[
            pl.BlockSpec((PROJ_TM, D_MODEL), lambda i, j: (i, 0)),
            pl.BlockSpec((D_MODEL, PROJ_TN), lambda i, j: (0, j + blk0)),
        ] + cast_specs,
        out_specs=[pl.BlockSpec((PROJ_TM, PROJ_TN), lambda i, j: (i, j))] + cast_specs,
        out_shape=[jax.ShapeDtypeStruct((m, width), bf16)]
        + [jax.ShapeDtypeStruct(w.shape, bf16) for w in cast],
        compiler_params=_cparams(("parallel", "arbitrary")),
        name=name,
    )(h, w_in_bf, *cast)


ATT_QROWS = 8
ATT_TQ = ATT_QROWS * GRID_W
ATT_WROWS = ATT_QROWS + WIN_ROWS - 1
ATT_TK = ATT_WROWS * GRID_W
ATT_NKEY = WIN_ROWS * GRID_W
ATT_RPI = 2
ATT_UNITS = ATT_RPI * N_PAIRS
ATT_AHEAD = 2
ATT_SLOTS = ATT_AHEAD + 1


def _attn_window_start(b):
    return jnp.clip(ATT_QROWS * b - WIN_ROWS // 2, 0, ROWS - ATT_WROWS)


def _attn_kernel(q_ref, k_ref, v_ref, bias_ref, o_ref, s_scr):
    b = pl.program_id(1)
    ws = _attn_window_start(b)
    lane = lax.broadcasted_iota(jnp.int32, (1, LANES), 1)
    lo = lane < HEAD_DIM

    def rows_body(it, carry):
        units = []
        for sub in range(ATT_RPI):
            i = it * ATT_RPI + sub
            r = ATT_QROWS * b + i
            rstart = jnp.clip(r - WIN_ROWS // 2, 0, ROWS - WIN_ROWS)
            off = pl.multiple_of((rstart - ws) * GRID_W, GRID_W)
            ro_base = rstart - r + (WIN_ROWS - 1)
            qoff = pl.multiple_of(i * GRID_W, GRID_W)
            for p in range(N_PAIRS):
                units.append((p, slice(p * LANES, (p + 1) * LANES), off, ro_base, qoff))

        tiles = [slice(t * LANES, (t + 1) * LANES) for t in range(ATT_NKEY // LANES)]
        slot0 = lax.shift_right_arithmetic(b, 31)
        row_max = {}

        def score_unit(u):
            p, cols, off, ro_base, qoff = units[u]
            qp = q_ref[pl.ds(qoff, GRID_W), cols]
            zero = jnp.zeros_like(qp)
            lhs = jnp.concatenate([jnp.where(lo, qp, zero), jnp.where(lo, zero, qp)], axis=0)
            kw = k_ref[pl.ds(off, ATT_NKEY), cols]
            s = lax.dot_general(lhs, kw, (((1,), (1,)), ((), ())),
                                preferred_element_type=f32)
            mx = None
            for t, tile in enumerate(tiles):
                st = s[:, tile] + bias_ref[p, ro_base + 2 * t]
                s_scr[slot0 + u % ATT_SLOTS, :, tile] = st
                mx = st if mx is None else jnp.maximum(mx, st)
            row_max[u] = jnp.max(mx, axis=-1, keepdims=True)

        def finish_unit(u):
            p, cols, off, ro_base, qoff = units[u]
            acc = None
            o = None
            for t, tile in enumerate(tiles):
                e = jnp.exp(s_scr[slot0 + u % ATT_SLOTS, :, tile] - row_max[u])
                acc = e if acc is None else acc + e
                vt = v_ref[pl.ds(off + t * LANES, LANES), cols]
                ot = jnp.dot(e.astype(bf16), vt, preferred_element_type=f32)
                o = ot if o is None else o + ot
            o = o / jnp.sum(acc, axis=-1, keepdims=True)
            o_ref[pl.ds(qoff, GRID_W), cols] = jnp.where(
                lo, o[:GRID_W], o[GRID_W:]).astype(bf16)

        for u in range(len(units) + ATT_AHEAD):
            if u < len(units):
                score_unit(u)
            if u >= ATT_AHEAD:
                finish_unit(u - ATT_AHEAD)
        return carry

    lax.fori_loop(0, ATT_QROWS // ATT_RPI, rows_body, 0)


def _attn_bias_table(rpb):
    c = np.arange(GRID_W)
    cstart = np.clip(c - WIN_COLS // 2, 0, GRID_W - WIN_COLS)
    kc = np.arange(GRID_W)
    valid = (kc[None, :] >= cstart[:, None]) & (kc[None, :] < cstart[:, None] + WIN_COLS)
    idx = kc[None, :] - c[:, None] + (WIN_COLS - 1)
    onehot = (valid[None] & (idx[None] == np.arange(2 * WIN_COLS - 1)[:, None, None]))
    t = jnp.einsum('hrj,jck->hrck', rpb.astype(f32), jnp.asarray(onehot, f32),
                   precision=lax.Precision.HIGHEST)
    t = t + jnp.asarray(np.where(valid, 0.0, NEG_BIAS), f32)
    t = t.reshape(N_PAIRS, 2, 2 * WIN_ROWS - 1, GRID_W, GRID_W)
    t = t.transpose(0, 2, 1, 3, 4).reshape(N_PAIRS, 2 * WIN_ROWS - 1, 2 * GRID_W, GRID_W)
    return jnp.concatenate([t[:, :-1], t[:, 1:]], axis=-1)


def _attn(zqk, zvf, bias_tab, batch):
    m = zqk.shape[0]
    nblk = ROWS // ATT_QROWS

    def kv_map(col):
        def index_map(s, b):
            return ((s * ROWS + _attn_window_start(b)) * GRID_W, col)
        return index_map

    return pl.pallas_call(
        _attn_kernel,
        grid=(batch, nblk),
        in_specs=[
            pl.BlockSpec((ATT_TQ, ATT_WIDTH), lambda s, b: (s * nblk + b, 0)),
            pl.BlockSpec((pl.Element(ATT_TK), pl.Element(ATT_WIDTH)), kv_map(ATT_WIDTH)),
            pl.BlockSpec((pl.Element(ATT_TK), pl.Element(ATT_WIDTH)), kv_map(0)),
            pl.BlockSpec(bias_tab.shape, lambda s, b: (0, 0, 0, 0)),
        ],
        out_specs=pl.BlockSpec((ATT_TQ, ATT_WIDTH), lambda s, b: (s * nblk + b, 0)),
        out_shape=jax.ShapeDtypeStruct((m, ATT_WIDTH), bf16),
        compiler_params=_cparams(("parallel", "arbitrary")),
        scratch_shapes=[pltpu.VMEM((ATT_SLOTS, 2 * GRID_W, ATT_NKEY), f32)],
        name="attn",
    )(zqk, zqk, zvf, bias_tab)


def _dft_kernel(m1_ref, m2_ref, m3_ref, x_ref, o_ref, y_ref):
    for c in range(256 // CHUNK):
        x16 = x_ref[0, :, c * CHUNK:(c + 1) * CHUNK, :].astype(f32)
        for hf in range(CHUNK // HALF):
            rows = slice(hf * HALF, (hf + 1) * HALF)
            xin = x16[:, rows, :].reshape(R1 * HALF, FGROUP_DIM).astype(bf16)
            res = jnp.dot(m1_ref[2 * c + hf], xin, preferred_element_type=f32)
            y_ref[:, :, c, rows, :] = res.reshape(2, R1, HALF, FGROUP_DIM)
    for k1 in range(R1):
        for hf in range(R3 // HALF):
            rows = slice(hf * HALF, (hf + 1) * HALF)
            xin = y_ref[:, k1, :, rows, :].reshape(2 * R2 * HALF, FGROUP_DIM).astype(bf16)
            res = jnp.dot(m2_ref[hf], xin, preferred_element_type=f32)
            y_ref[:, k1, :, rows, :] = res.reshape(2, R2, HALF, FGROUP_DIM)
    for h in range(R1 // CHUNK):
        k1s = slice(h * CHUNK, (h + 1) * CHUNK)
        for k2 in range(R2):
            xin = y_ref[:, k1s, k2, :, :].reshape(2 * CHUNK * R3, FGROUP_DIM).astype(bf16)
            res = jnp.dot(m3_ref[...], xin, preferred_element_type=f32)
            o_ref[0, :, :, k2, k1s, :] = res.reshape(2, R3, CHUNK, FGROUP_DIM).astype(bf16)


def _dft(zvf, m1, m2, m3, batch):
    zv = zvf.reshape(batch, R1, 256, ATT_WIDTH + FOURIER_WIDTH)
    f_blk0 = ATT_WIDTH // FGROUP_DIM
    const = lambda a: pl.BlockSpec(a.shape, lambda s, g: (0,) * a.ndim,
                                   pipeline_mode=pl.Buffered(1))
    return pl.pallas_call(
        _dft_kernel,
        grid=(batch, N_FGROUPS),
        in_specs=[
            const(m1), const(m2), const(m3),
            pl.BlockSpec((1, R1, 256, FGROUP_DIM), lambda s, g: (s, 0, 0, f_blk0 + g)),
        ],
        out_specs=pl.BlockSpec((1, 2, R3, R2, R1, FGROUP_DIM), lambda s, g: (s, 0, 0, 0, 0, g)),
        out_shape=jax.ShapeDtypeStruct((batch, 2, R3, R2, R1, FOURIER_WIDTH), bf16),
        scratch_shapes=[pltpu.VMEM((2, R1, R2, R3, FGROUP_DIM), f32)],
        compiler_params=_cparams(("parallel", "arbitrary")),
        name="dft",
    )(m1, m2, m3, zv)


MERGE_TM = 512


def _merge_kernel(p_ref, att_ref, sa_ref, sb_ref, x_ref, cc_ref, sc_ref,
                  wfo_ref, wao_ref, wout_ref, o_ref):
    fm = []
    for g in range(N_FGROUPS):
        cols = slice(g * FGROUP_DIM, (g + 1) * FGROUP_DIM)
        a = jnp.dot(p_ref[0, 0, :, cols], cc_ref[...], preferred_element_type=f32)
        a = a + jnp.dot(p_ref[0, 1, :, cols], sc_ref[...], preferred_element_type=f32)
        fm.append(a.astype(bf16))
    fm = jnp.concatenate(fm, axis=1)
    ya = jnp.dot(fm, wfo_ref[...], preferred_element_type=f32)
    yb = jnp.dot(att_ref[...], wao_ref[...], preferred_element_type=f32)
    merged = sa_ref[...].astype(f32) * ya + sb_ref[...].astype(f32) * yb
    o_ref[...] = x_ref[...] + jnp.dot(merged.astype(bf16), wout_ref[...],
                                      preferred_element_type=f32)


def _merge(pz, att, zg, x2d, cc, sc, wfo, wao, wout, batch):
    m = x2d.shape[0]
    per_seq = SEQ // MERGE_TM
    pv = pz.reshape(batch, 2, SEQ, FOURIER_WIDTH)
    const = lambda shape: pl.BlockSpec(shape, lambda i: (0, 0), pipeline_mode=pl.Buffered(1))
    return pl.pallas_call(
        _merge_kernel,
        grid=(m // MERGE_TM,),
        in_specs=[
            pl.BlockSpec((1, 2, MERGE_TM, FOURIER_WIDTH),
                         lambda i: (i // per_seq, 0, i % per_seq, 0)),
            pl.BlockSpec((MERGE_TM, ATT_WIDTH), lambda i: (i, 0)),
            pl.BlockSpec((MERGE_TM, D_MODEL), lambda i: (i, 0)),
            pl.BlockSpec((MERGE_TM, D_MODEL), lambda i: (i, 1)),
            pl.BlockSpec((MERGE_TM, D_MODEL), lambda i: (i, 0)),
            const(cc.shape), const(sc.shape), const(wfo.shape), const(wao.shape), const(wout.shape),
        ],
        out_specs=pl.BlockSpec((MERGE_TM, D_MODEL), lambda i: (i, 0)),
        out_shape=jax.ShapeDtypeStruct((m, D_MODEL), f32),
        compiler_params=_cparams(("parallel",)),
        name="merge",
    )(pv, att, zg, zg, x2d, cc, sc, wfo, wao, wout)


MLP_TM = 512
MLP_TF = 1024
MLP_CH = 256


def _mlp_kernel(x_ref, g_ref, w1_ref, w2_ref, o_ref, h_ref):
    j = pl.program_id(1)

    @pl.when(j == 0)
    def _():
        x = x_ref[...]
        h_ref[...] = _rmsnorm_rows(x, g_ref[...]).astype(bf16)
        o_ref[...] = x

    h = h_ref[...]
    us = []
    for c in range(MLP_TF // MLP_CH):
        u = jnp.dot(h, w1_ref[:, c * MLP_CH:(c + 1) * MLP_CH], preferred_element_type=f32)
        u = jnp.maximum(u, 0.0)
        us.append((u * u).astype(bf16))
    o_ref[...] += jnp.dot(jnp.concatenate(us, axis=1), w2_ref[...], preferred_element_type=f32)


def _mlp(x1, g_mlp, w1_bf, w2_bf):
    m = x1.shape[0]
    return pl.pallas_call(
        _mlp_kernel,
        grid=(m // MLP_TM, D_FF // MLP_TF),
        in_specs=[
            pl.BlockSpec((MLP_TM, D_MODEL), lambda i, j: (i, 0)),
            pl.BlockSpec((1, D_MODEL), lambda i, j: (0, 0)),
            pl.BlockSpec((D_MODEL, MLP_TF), lambda i, j: (0, j)),
            pl.BlockSpec((MLP_TF, D_MODEL), lambda i, j: (j, 0)),
        ],
        out_specs=pl.BlockSpec((MLP_TM, D_MODEL), lambda i, j: (i, 0)),
        out_shape=jax.ShapeDtypeStruct((m, D_MODEL), f32),
        scratch_shapes=[pltpu.VMEM((MLP_TM, D_MODEL), bf16)],
        compiler_params=_cparams(("parallel", "arbitrary")),
        name="mlp",
    )(x1, g_mlp, w1_bf, w2_bf)


def _encoder_block(x, consts, params, late_weights):
    batch, seq, _ = x.shape
    assert seq == SEQ
    m2, m3, cc, sc = consts
    g_mix, w_qk, qk_gain, bias_tab, g_mlp = params
    to_cast = late_weights[0].dtype != bf16
    x2d = x.reshape(batch * seq, D_MODEL)
    if to_cast:
        zqk, h, w_in = _proj_qk(x2d, g_mix, w_qk, qk_gain, cast=late_weights[:1])
    else:
        w_in = late_weights[0]
        zqk, h = _proj_qk(x2d, g_mix, w_in, qk_gain)
    zvf, *mix_w = _proj_act(h, w_in, 2 * ATT_WIDTH, ATT_WIDTH + FOURIER_WIDTH, False, "proj_vf",
                            cast=late_weights[1:5] if to_cast else ())
    zg, *mlp_w = _proj_act(h, w_in, 3 * ATT_WIDTH + FOURIER_WIDTH, 2 * D_MODEL, True, "proj_gate",
                           cast=late_weights[5:] if to_cast else ())
    if to_cast:
        late_weights = (w_in,) + tuple(mix_w) + tuple(mlp_w)
    _, w_fo, w_ao, w_out, m1, w1, w2 = late_weights
    att = _attn(zqk, zvf, bias_tab, batch)
    pz = _dft(zvf, m1.reshape(256 // HALF, 2 * R1 * HALF, R1 * HALF), m2, m3, batch)
    x1 = _merge(pz, att, zg, x2d, cc, sc, w_fo, w_ao, w_out, batch)
    out = _mlp(x1, g_mlp, w1, w2)
    return out.reshape(batch, seq, D_MODEL), late_weights


def kernel(x_prompt, x_sample, g_mix, w_in, q_norm, k_norm, rpb, w_fo, w_ao, w_out, g_mlp, w1, w2):
    cc_np, sc_np = _channel_matrices()
    consts = tuple(
        jnp.asarray(a, f32).astype(bf16)
        for a in (_stage2_matrices(), _stage3_matrix(), cc_np, sc_np))
    m1 = jnp.asarray(_stage1_matrices().reshape(-1, R1 * HALF), f32)
    y_prompt, y_sample = x_prompt, x_sample
    for l in range(g_mix.shape[0]):
        qk_gain = jnp.stack([
            jnp.tile(q_norm[l].astype(f32), 2) * (HEAD_DIM ** -0.5),
            jnp.tile(k_norm[l].astype(f32), 2),
        ])
        params = (
            g_mix[l].astype(f32).reshape(1, D_MODEL), w_in[l][:, :2 * ATT_WIDTH].astype(bf16),
            qk_gain, _attn_bias_table(rpb[l]), g_mlp[l].astype(f32).reshape(1, D_MODEL),
        )
        late_weights = (w_in[l].astype(f32), w_fo[l].astype(f32), w_ao[l].astype(f32),
                        w_out[l].astype(f32), m1, w1[l].astype(f32), w2[l].astype(f32))
        y_prompt, late_weights = _encoder_block(y_prompt, consts, params, late_weights)
        y_sample, late_weights = _encoder_block(y_sample, consts, params, late_weights)
    return (y_prompt, y_sample)
```

```python
import functools

import numpy as np
import jax
import jax.numpy as jnp
from jax import lax
from jax.experimental import pallas as pl
from jax.experimental.pallas import tpu as pltpu

D_MODEL = 2048
GRID_W = 64
N_HEADS = 16
HEAD_DIM = 64
ATT_WIDTH = N_HEADS * HEAD_DIM
N_FGROUPS = 4
FGROUP_DIM = 256
FOURIER_WIDTH = N_FGROUPS * FGROUP_DIM
WIN_ROWS = 8
WIN_COLS = 16
D_FF = 4 * D_MODEL
IN_WIDTH = 3 * ATT_WIDTH + FOURIER_WIDTH + 2 * D_MODEL
EPS = 1e-6

SEQ = 8192
ROWS = SEQ // GRID_W
LANES = 128
N_PAIRS = N_HEADS // 2
NEG_BIAS = -1e30

R1, R2, R3 = 32, 16, 16
CHUNK = 16
HALF = 8

VMEM_LIMIT = 56 * 1024 * 1024

bf16 = jnp.bfloat16
f32 = jnp.float32


def _cparams(sem):
    return pltpu.CompilerParams(dimension_semantics=sem, vmem_limit_bytes=VMEM_LIMIT)


def _stage1_matrices():
    mc = np.arange(256 // HALF)[:, None, None, None]
    k1 = np.arange(R1)[None, :, None, None]
    ml = np.arange(HALF)[None, None, :, None]
    n1 = np.arange(R1)[None, None, None, :]
    theta = 2.0 * np.pi * (((256 * n1 + HALF * mc + ml) * k1) % SEQ) / SEQ
    eye = np.eye(HALF)
    scale = 1.0 / np.sqrt(R1)
    out = np.zeros((256 // HALF, 2, R1, HALF, R1, HALF), np.float64)
    for part, fn in enumerate((np.cos, lambda t: -np.sin(t))):
        vals = fn(theta) * scale
        out[:, part] = vals[:, :, :, :, None] * eye[None, None, :, None, :]
    return out.reshape(256 // HALF, 2 * R1 * HALF, R1 * HALF)


def _complex_block(cos, sin):
    return np.block([[cos, sin], [-sin, cos]])


def _stage2_matrices():
    mats = []
    for hf in range(R3 // HALF):
        k2 = np.arange(R2)[:, None, None, None]
        n3o = np.arange(HALF)[None, :, None, None]
        n2 = np.arange(R2)[None, None, :, None]
        n3 = np.arange(HALF)[None, None, None, :]
        phi = 2.0 * np.pi * (((16 * n2 + HALF * hf + n3) * k2) % 256) / 256
        delta = (n3o == n3).astype(np.float64)
        scale = 1.0 / np.sqrt(R2)
        cos = (np.cos(phi) * delta * scale).reshape(R2 * HALF, R2 * HALF)
        sin = (np.sin(phi) * delta * scale).reshape(R2 * HALF, R2 * HALF)
        mats.append(_complex_block(cos, sin))
    return np.stack(mats)


def _stage3_matrix():
    k3 = np.arange(R3)[:, None, None, None]
    k1o = np.arange(HALF)[None, :, None, None]
    k1 = np.arange(HALF)[None, None, :, None]
    n3 = np.arange(R3)[None, None, None, :]
    psi = 2.0 * np.pi * ((n3 * k3) % R3) / R3
    delta = (k1o == k1).astype(np.float64)
    scale = 1.0 / np.sqrt(R3)
    cos = (np.cos(psi) * delta * scale).reshape(R3 * HALF, HALF * R3)
    sin = (np.sin(psi) * delta * scale).reshape(R3 * HALF, HALF * R3)
    return _complex_block(cos, sin)


def _channel_matrices():
    c = np.arange(FGROUP_DIM)
    ang = 2.0 * np.pi * ((c[:, None] * c[None, :]) % FGROUP_DIM) / FGROUP_DIM
    scale = 1.0 / np.sqrt(FGROUP_DIM)
    return np.cos(ang) * scale, np.sin(ang) * scale


PROJ_TM = 1024
PROJ_TN = 1024


def _rmsnorm_rows(x, g):
    ms = jnp.mean(x * x, axis=-1, keepdims=True)
    return x * lax.rsqrt(ms + EPS) * g


PROJ_CH = 256


def _proj_qk_kernel(x_ref, g_ref, w_ref, gain_ref, *refs, n_cast):
    cast_in, (o_ref, h_ref), cast_out = refs[:n_cast], refs[n_cast:n_cast + 2], refs[n_cast + 2:]
    for src, dst in zip(cast_in, cast_out):
        dst[...] = src[...].astype(bf16)
    j = pl.program_id(1)

    @pl.when(j == 0)
    def _():
        h_ref[...] = _rmsnorm_rows(x_ref[...], g_ref[...]).astype(bf16)

    gain = gain_ref[pl.ds(j, 1), :]
    lane = lax.broadcasted_iota(jnp.int32, (1, LANES), 1)
    lo = lane < HEAD_DIM
    h = h_ref[...]
    for c in range(PROJ_TN // PROJ_CH):
        acc = jnp.dot(h, w_ref[:, c * PROJ_CH:(c + 1) * PROJ_CH], preferred_element_type=f32)
        for p in range(PROJ_CH // LANES):
            a = acc[:, p * LANES:(p + 1) * LANES]
            sq = a * a
            s0 = jnp.sum(jnp.where(lo, sq, 0.0), axis=-1, keepdims=True)
            s1 = jnp.sum(jnp.where(lo, 0.0, sq), axis=-1, keepdims=True)
            r0 = lax.rsqrt(s0 * (1.0 / HEAD_DIM) + EPS)
            r1 = lax.rsqrt(s1 * (1.0 / HEAD_DIM) + EPS)
            col = c * PROJ_CH + p * LANES
            o_ref[:, col:col + LANES] = (a * jnp.where(lo, r0, r1) * gain).astype(bf16)


def _proj_act_kernel(h_ref, w_ref, *refs, sigmoid, n_cast):
    cast_in, o_ref, cast_out = refs[:n_cast], refs[n_cast], refs[n_cast + 1:]
    h = h_ref[...]
    for c in range(PROJ_TN // PROJ_CH):
        cols = slice(c * PROJ_CH, (c + 1) * PROJ_CH)
        acc = jnp.dot(h, w_ref[:, cols], preferred_element_type=f32)
        if sigmoid:
            acc = 0.5 * jnp.tanh(0.5 * acc) + 0.5
        o_ref[:, cols] = acc.astype(bf16)
    for src, dst in zip(cast_in, cast_out):
        dst[...] = src[...].astype(bf16)


def _cast_specs(cast, grid):
    nsteps = grid[0] * grid[1]
    specs = []
    for w in cast:
        rows = w.shape[0] // nsteps
        assert rows * nsteps == w.shape[0] and rows % CHUNK == 0
        specs.append(pl.BlockSpec((rows, w.shape[1]), lambda i, j: (i * grid[1] + j, 0)))
    return specs


def _proj_qk(x2d, g_mix, w_qk_bf, qk_gain, cast=()):
    m = x2d.shape[0]
    grid = (m // PROJ_TM, 2 * ATT_WIDTH // PROJ_TN)
    cast_specs = _cast_specs(cast, grid)
    return pl.pallas_call(
        functools.partial(_proj_qk_kernel, n_cast=len(cast)),
        grid=grid,
        in_specs=[
            pl.BlockSpec((PROJ_TM, D_MODEL), lambda i, j: (i, 0)),
            pl.BlockSpec((1, D_MODEL), lambda i, j: (0, 0)),
            pl.BlockSpec((D_MODEL, PROJ_TN), lambda i, j: (0, j)),
            pl.BlockSpec((2, LANES), lambda i, j: (0, 0)),
        ] + cast_specs,
        out_specs=[pl.BlockSpec((PROJ_TM, PROJ_TN), lambda i, j: (i, j)),
                   pl.BlockSpec((PROJ_TM, D_MODEL), lambda i, j: (i, 0))] + cast_specs,
        out_shape=[jax.ShapeDtypeStruct((m, 2 * ATT_WIDTH), bf16),
                   jax.ShapeDtypeStruct((m, D_MODEL), bf16)]
        + [jax.ShapeDtypeStruct(w.shape, bf16) for w in cast],
        compiler_params=_cparams(("parallel", "arbitrary")),
        name="proj_qk",
    )(x2d, g_mix, w_qk_bf, qk_gain, *cast)


def _proj_act(h, w_in_bf, col0, width, sigmoid, name, cast=()):
    m = h.shape[0]
    blk0 = col0 // PROJ_TN
    grid = (m // PROJ_TM, width // PROJ_TN)
    cast_specs = _cast_specs(cast, grid)
    return pl.pallas_call(
        functools.partial(_proj_act_kernel, sigmoid=sigmoid, n_cast=len(cast)),
        grid=grid,
        in_specs=[
            pl.BlockSpec((PROJ_TM, D_MODEL), lambda i, j: (i, 0)),
            pl.BlockSpec((D_MODEL, PROJ_TN), lambda i, j: (0, j + blk0)),
        ] + cast_specs,
        out_specs=[pl.BlockSpec((PROJ_TM, PROJ_TN), lambda i, j: (i, j))] + cast_specs,
        out_shape=[jax.ShapeDtypeStruct((m, width), bf16)]
        + [jax.ShapeDtypeStruct(w.shape, bf16) for w in cast],
        compiler_params=_cparams(("parallel", "arbitrary")),
        name=name,
    )(h, w_in_bf, *cast)


ATT_QROWS = 8
ATT_TQ = ATT_QROWS * GRID_W
ATT_WROWS = ATT_QROWS + WIN_ROWS - 1
ATT_TK = ATT_WROWS * GRID_W
ATT_NKEY = WIN_ROWS * GRID_W
ATT_RPI = 2
ATT_UNITS = ATT_RPI * N_PAIRS
ATT_AHEAD = 2
ATT_SLOTS = ATT_AHEAD + 1


def _attn_window_start(b):
    return jnp.clip(ATT_QROWS * b - WIN_ROWS // 2, 0, ROWS - ATT_WROWS)


def _attn_kernel(q_ref, k_ref, v_ref, bias_ref, o_ref, s_scr):
    b = pl.program_id(1)
    ws = _attn_window_start(b)
    lane = lax.broadcasted_iota(jnp.int32, (1, LANES), 1)
    lo = lane < HEAD_DIM

    def rows_body(it, carry):
        units = []
        for sub in range(ATT_RPI):
            i = it * ATT_RPI + sub
            r = ATT_QROWS * b + i
            rstart = jnp.clip(r - WIN_ROWS // 2, 0, ROWS - WIN_ROWS)
            off = pl.multiple_of((rstart - ws) * GRID_W, GRID_W)
            ro_base = rstart - r + (WIN_ROWS - 1)
            qoff = pl.multiple_of(i * GRID_W, GRID_W)
            for p in range(N_PAIRS):
                units.append((p, slice(p * LANES, (p + 1) * LANES), off, ro_base, qoff))

        tiles = [slice(t * LANES, (t + 1) * LANES) for t in range(ATT_NKEY // LANES)]
        slot0 = lax.shift_right_arithmetic(b, 31)
        row_max = {}

        def score_unit(u):
            p, cols, off, ro_base, qoff = units[u]
            qp = q_ref[pl.ds(qoff, GRID_W), cols]
            zero = jnp.zeros_like(qp)
            lhs = jnp.concatenate([jnp.where(lo, qp, zero), jnp.where(lo, zero, qp)], axis=0)
            kw = k_ref[pl.ds(off, ATT_NKEY), cols]
            s = lax.dot_general(lhs, kw, (((1,), (1,)), ((), ())),
                                preferred_element_type=f32)
            mx = None
            for t, tile in enumerate(tiles):
                st = s[:, tile] + bias_ref[p, ro_base + 2 * t]
                s_scr[slot0 + u % ATT_SLOTS, :, tile] = st
                mx = st if mx is None else jnp.maximum(mx, st)
            row_max[u] = jnp.max(mx, axis=-1, keepdims=True)

        def finish_unit(u):
            p, cols, off, ro_base, qoff = units[u]
            acc = None
            o = None
            for t, tile in enumerate(tiles):
                e = jnp.exp(s_scr[slot0 + u % ATT_SLOTS, :, tile] - row_max[u])
                acc = e if acc is None else acc + e
                vt = v_ref[pl.ds(off + t * LANES, LANES), cols]
                ot = jnp.dot(e.astype(bf16), vt, preferred_element_type=f32)
                o = ot if o is None else o + ot
            o = o / jnp.sum(acc, axis=-1, keepdims=True)
            o_ref[pl.ds(qoff, GRID_W), cols] = jnp.where(
                lo, o[:GRID_W], o[GRID_W:]).astype(bf16)

        for u in range(len(units) + ATT_AHEAD):
            if u < len(units):
                score_unit(u)
            if u >= ATT_AHEAD:
                finish_unit(u - ATT_AHEAD)
        return carry

    lax.fori_loop(0, ATT_QROWS // ATT_RPI, rows_body, 0)


def _attn_bias_table(rpb):
    c = np.arange(GRID_W)
    cstart = np.clip(c - WIN_COLS // 2, 0, GRID_W - WIN_COLS)
    kc = np.arange(GRID_W)
    valid = (kc[None, :] >= cstart[:, None]) & (kc[None, :] < cstart[:, None] + WIN_COLS)
    idx = kc[None, :] - c[:, None] + (WIN_COLS - 1)
    onehot = (valid[None] & (idx[None] == np.arange(2 * WIN_COLS - 1)[:, None, None]))
    t = jnp.einsum('hrj,jck->hrck', rpb.astype(f32), jnp.asarray(onehot, f32),
                   precision=lax.Precision.HIGHEST)
    t = t + jnp.asarray(np.where(valid, 0.0, NEG_BIAS), f32)
    t = t.reshape(N_PAIRS, 2, 2 * WIN_ROWS - 1, GRID_W, GRID_W)
    t = t.transpose(0, 2, 1, 3, 4).reshape(N_PAIRS, 2 * WIN_ROWS - 1, 2 * GRID_W, GRID_W)
    return jnp.concatenate([t[:, :-1], t[:, 1:]], axis=-1)


def _attn(zqk, zvf, bias_tab, batch):
    m = zqk.shape[0]
    nblk = ROWS // ATT_QROWS

    def kv_map(col):
        def index_map(s, b):
            return ((s * ROWS + _attn_window_start(b)) * GRID_W, col)
        return index_map

    return pl.pallas_call(
        _attn_kernel,
        grid=(batch, nblk),
        in_specs=[
            pl.BlockSpec((ATT_TQ, ATT_WIDTH), lambda s, b: (s * nblk + b, 0)),
            pl.BlockSpec((pl.Element(ATT_TK), pl.Element(ATT_WIDTH)), kv_map(ATT_WIDTH)),
            pl.BlockSpec((pl.Element(ATT_TK), pl.Element(ATT_WIDTH)), kv_map(0)),
            pl.BlockSpec(bias_tab.shape, lambda s, b: (0, 0, 0, 0)),
        ],
        out_specs=pl.BlockSpec((ATT_TQ, ATT_WIDTH), lambda s, b: (s * nblk + b, 0)),
        out_shape=jax.ShapeDtypeStruct((m, ATT_WIDTH), bf16),
        compiler_params=_cparams(("parallel", "arbitrary")),
        scratch_shapes=[pltpu.VMEM((ATT_SLOTS, 2 * GRID_W, ATT_NKEY), f32)],
        name="attn",
    )(zqk, zqk, zvf, bias_tab)


def _dft_kernel(m1_ref, m2_ref, m3_ref, x_ref, o_ref, y_ref):
    for c in range(256 // CHUNK):
        x16 = x_ref[0, :, c * CHUNK:(c + 1) * CHUNK, :].astype(f32)
        for hf in range(CHUNK // HALF):
            rows = slice(hf * HALF, (hf + 1) * HALF)
            xin = x16[:, rows, :].reshape(R1 * HALF, FGROUP_DIM).astype(bf16)
            res = jnp.dot(m1_ref[2 * c + hf], xin, preferred_element_type=f32)
            y_ref[:, :, c, rows, :] = res.reshape(2, R1, HALF, FGROUP_DIM)
    for k1 in range(R1):
        for hf in range(R3 // HALF):
            rows = slice(hf * HALF, (hf + 1) * HALF)
            xin = y_ref[:, k1, :, rows, :].reshape(2 * R2 * HALF, FGROUP_DIM).astype(bf16)
            res = jnp.dot(m2_ref[hf], xin, preferred_element_type=f32)
            y_ref[:, k1, :, rows, :] = res.reshape(2, R2, HALF, FGROUP_DIM)
    for h in range(R1 // CHUNK):
        for k2 in range(R2):
            halves = []
            for hf in range(CHUNK // HALF):
                k1s = slice(h * CHUNK + hf * HALF, h * CHUNK + (hf + 1) * HALF)
                xin = y_ref[:, k1s, k2, :, :].reshape(2 * HALF * R3, FGROUP_DIM).astype(bf16)
                res = jnp.dot(m3_ref[...], xin, preferred_element_type=f32)
                halves.append(res.reshape(2, R3, HALF, FGROUP_DIM))
            o_ref[0, :, :, k2, h * CHUNK:(h + 1) * CHUNK, :] = jnp.concatenate(
                halves, axis=2).astype(bf16)


def _dft(zvf, m1, m2, m3, batch):
    zv = zvf.reshape(batch, R1, 256, ATT_WIDTH + FOURIER_WIDTH)
    f_blk0 = ATT_WIDTH // FGROUP_DIM
    const = lambda a: pl.BlockSpec(a.shape, lambda s, g: (0,) * a.ndim,
                                   pipeline_mode=pl.Buffered(1))
    return pl.pallas_call(
        _dft_kernel,
        grid=(batch, N_FGROUPS),
        in_specs=[
            const(m1), const(m2), const(m3),
            pl.BlockSpec((1, R1, 256, FGROUP_DIM), lambda s, g: (s, 0, 0, f_blk0 + g)),
        ],
        out_specs=pl.BlockSpec((1, 2, R3, R2, R1, FGROUP_DIM), lambda s, g: (s, 0, 0, 0, 0, g)),
        out_shape=jax.ShapeDtypeStruct((batch, 2, R3, R2, R1, FOURIER_WIDTH), bf16),
        scratch_shapes=[pltpu.VMEM((2, R1, R2, R3, FGROUP_DIM), f32)],
        compiler_params=_cparams(("parallel", "arbitrary")),
        name="dft",
    )(m1, m2, m3, zv)


MERGE_TM = 512


def _merge_kernel(p_ref, att_ref, sa_ref, sb_ref, x_ref, cc_ref, sc_ref,
                  wfo_ref, wao_ref, wout_ref, o_ref):
    fm = []
    for g in range(N_FGROUPS):
        cols = slice(g * FGROUP_DIM, (g + 1) * FGROUP_DIM)
        a = jnp.dot(p_ref[0, 0, :, cols], cc_ref[...], preferred_element_type=f32)
        a = a + jnp.dot(p_ref[0, 1, :, cols], sc_ref[...], preferred_element_type=f32)
        fm.append(a.astype(bf16))
    fm = jnp.concatenate(fm, axis=1)
    ya = jnp.dot(fm, wfo_ref[...], preferred_element_type=f32)
    yb = jnp.dot(att_ref[...], wao_ref[...], preferred_element_type=f32)
    merged = sa_ref[...].astype(f32) * ya + sb_ref[...].astype(f32) * yb
    o_ref[...] = x_ref[...] + jnp.dot(merged.astype(bf16), wout_ref[...],
                                      preferred_element_type=f32)


def _merge(pz, att, zg, x2d, cc, sc, wfo, wao, wout, batch):
    m = x2d.shape[0]
    per_seq = SEQ // MERGE_TM
    pv = pz.reshape(batch, 2, SEQ, FOURIER_WIDTH)
    const = lambda shape: pl.BlockSpec(shape, lambda i: (0, 0), pipeline_mode=pl.Buffered(1))
    return pl.pallas_call(
        _merge_kernel,
        grid=(m // MERGE_TM,),
        in_specs=[
            pl.BlockSpec((1, 2, MERGE_TM, FOURIER_WIDTH),
                         lambda i: (i // per_seq, 0, i % per_seq, 0)),
            pl.BlockSpec((MERGE_TM, ATT_WIDTH), lambda i: (i, 0)),
            pl.BlockSpec((MERGE_TM, D_MODEL), lambda i: (i, 0)),
            pl.BlockSpec((MERGE_TM, D_MODEL), lambda i: (i, 1)),
            pl.BlockSpec((MERGE_TM, D_MODEL), lambda i: (i, 0)),
            const(cc.shape), const(sc.shape), const(wfo.shape), const(wao.shape), const(wout.shape),
        ],
        out_specs=pl.BlockSpec((MERGE_TM, D_MODEL), lambda i: (i, 0)),
        out_shape=jax.ShapeDtypeStruct((m, D_MODEL), f32),
        compiler_params=_cparams(("parallel",)),
        name="merge",
    )(pv, att, zg, zg, x2d, cc, sc, wfo, wao, wout)


MLP_TM = 512
MLP_TF = 1024
MLP_CH = 256


def _mlp_kernel(x_ref, g_ref, w1_ref, w2_ref, o_ref, h_ref):
    j = pl.program_id(1)

    @pl.when(j == 0)
    def _():
        x = x_ref[...]
        h_ref[...] = _rmsnorm_rows(x, g_ref[...]).astype(bf16)
        o_ref[...] = x

    h = h_ref[...]
    us = []
    for c in range(MLP_TF // MLP_CH):
        u = jnp.dot(h, w1_ref[:, c * MLP_CH:(c + 1) * MLP_CH], preferred_element_type=f32)
        u = jnp.maximum(u, 0.0)
        us.append((u * u).astype(bf16))
    o_ref[...] += jnp.dot(jnp.concatenate(us, axis=1), w2_ref[...], preferred_element_type=f32)


def _mlp(x1, g_mlp, w1_bf, w2_bf):
    m = x1.shape[0]
    return pl.pallas_call(
        _mlp_kernel,
        grid=(m // MLP_TM, D_FF // MLP_TF),
        in_specs=[
            pl.BlockSpec((MLP_TM, D_MODEL), lambda i, j: (i, 0)),
            pl.BlockSpec((1, D_MODEL), lambda i, j: (0, 0)),
            pl.BlockSpec((D_MODEL, MLP_TF), lambda i, j: (0, j)),
            pl.BlockSpec((MLP_TF, D_MODEL), lambda i, j: (j, 0)),
        ],
        out_specs=pl.BlockSpec((MLP_TM, D_MODEL), lambda i, j: (i, 0)),
        out_shape=jax.ShapeDtypeStruct((m, D_MODEL), f32),
        scratch_shapes=[pltpu.VMEM((MLP_TM, D_MODEL), bf16)],
        compiler_params=_cparams(("parallel", "arbitrary")),
        name="mlp",
    )(x1, g_mlp, w1_bf, w2_bf)


def _encoder_block(x, consts, params, late_weights):
    batch, seq, _ = x.shape
    assert seq == SEQ
    m2, m3, cc, sc = consts
    g_mix, w_qk, qk_gain, bias_tab, g_mlp = params
    to_cast = late_weights[0].dtype != bf16
    x2d = x.reshape(batch * seq, D_MODEL)
    if to_cast:
        zqk, h, w_in = _proj_qk(x2d, g_mix, w_qk, qk_gain, cast=late_weights[:1])
    else:
        w_in = late_weights[0]
        zqk, h = _proj_qk(x2d, g_mix, w_in, qk_gain)
    zvf, *mix_w = _proj_act(h, w_in, 2 * ATT_WIDTH, ATT_WIDTH + FOURIER_WIDTH, False, "proj_vf",
                            cast=late_weights[1:5] if to_cast else ())
    zg, *mlp_w = _proj_act(h, w_in, 3 * ATT_WIDTH + FOURIER_WIDTH, 2 * D_MODEL, True, "proj_gate",
                           cast=late_weights[5:] if to_cast else ())
    if to_cast:
        late_weights = (w_in,) + tuple(mix_w) + tuple(mlp_w)
    _, w_fo, w_ao, w_out, m1, w1, w2 = late_weights
    att = _attn(zqk, zvf, bias_tab, batch)
    pz = _dft(zvf, m1.reshape(256 // HALF, 2 * R1 * HALF, R1 * HALF), m2, m3, batch)
    x1 = _merge(pz, att, zg, x2d, cc, sc, w_fo, w_ao, w_out, batch)
    out = _mlp(x1, g_mlp, w1, w2)
    return out.reshape(batch, seq, D_MODEL), late_weights


def kernel(x_prompt, x_sample, g_mix, w_in, q_norm, k_norm, rpb, w_fo, w_ao, w_out, g_mlp, w1, w2):
    cc_np, sc_np = _channel_matrices()
    consts = tuple(
        jnp.asarray(a, f32).astype(bf16)
        for a in (_stage2_matrices(), _stage3_matrix(), cc_np, sc_np))
    m1 = jnp.asarray(_stage1_matrices().reshape(-1, R1 * HALF), f32)
    y_prompt, y_sample = x_prompt, x_sample
    for l in range(g_mix.shape[0]):
        qk_gain = jnp.stack([
            jnp.tile(q_norm[l].astype(f32), 2) * (HEAD_DIM ** -0.5),
            jnp.tile(k_norm[l].astype(f32), 2),
        ])
        params = (
            g_mix[l].astype(f32).reshape(1, D_MODEL), w_in[l][:, :2 * ATT_WIDTH].astype(bf16),
            qk_gain, _attn_bias_table(rpb[l]), g_mlp[l].astype(f32).reshape(1, D_MODEL),
        )
        late_weights = (w_in[l].astype(f32), w_fo[l].astype(f32), w_ao[l].astype(f32),
                        w_out[l].astype(f32), m1, w1[l].astype(f32), w2[l].astype(f32))
        y_prompt, late_weights = _encoder_block(y_prompt, consts, params, late_weights)
        y_sample, late_weights = _encoder_block(y_sample, consts, params, late_weights)
    return (y_prompt, y_sample)
```

```python
import functools

import numpy as np
import jax
import jax.numpy as jnp
from jax import lax
from jax.experimental import pallas as pl
from jax.experimental.pallas import tpu as pltpu

D_MODEL = 2048
GRID_W = 64
N_HEADS = 16
HEAD_DIM = 64
ATT_WIDTH = N_HEADS * HEAD_DIM
N_FGROUPS = 4
FGROUP_DIM = 256
FOURIER_WIDTH = N_FGROUPS * FGROUP_DIM
WIN_ROWS = 8
WIN_COLS = 16
D_FF = 4 * D_MODEL
IN_WIDTH = 3 * ATT_WIDTH + FOURIER_WIDTH + 2 * D_MODEL
EPS = 1e-6

SEQ = 8192
ROWS = SEQ // GRID_W
LANES = 128
N_PAIRS = N_HEADS // 2
NEG_BIAS = -1e30

R1, R2, R3 = 32, 16, 16
CHUNK = 16
HALF = 8

VMEM_LIMIT = 56 * 1024 * 1024

bf16 = jnp.bfloat16
f32 = jnp.float32


def _cparams(sem):
    return pltpu.CompilerParams(dimension_semantics=sem, vmem_limit_bytes=VMEM_LIMIT)


def _stage1_matrices():
    mc = np.arange(256 // HALF)[:, None, None, None]
    k1 = np.arange(R1)[None, :, None, None]
    ml = np.arange(HALF)[None, None, :, None]
    n1 = np.arange(R1)[None, None, None, :]
    theta = 2.0 * np.pi * (((256 * n1 + HALF * mc + ml) * k1) % SEQ) / SEQ
    eye = np.eye(HALF)
    scale = 1.0 / np.sqrt(R1)
    out = np.zeros((256 // HALF, 2, R1, HALF, R1, HALF), np.float64)
    for part, fn in enumerate((np.cos, lambda t: -np.sin(t))):
        vals = fn(theta) * scale
        out[:, part] = vals[:, :, :, :, None] * eye[None, None, :, None, :]
    return out.reshape(256 // HALF, 2 * R1 * HALF, R1 * HALF)


def _complex_block(cos, sin):
    return np.block([[cos, sin], [-sin, cos]])


def _stage2_matrices():
    mats = []
    for hf in range(R3 // HALF):
        k2 = np.arange(R2)[:, None, None, None]
        n3o = np.arange(HALF)[None, :, None, None]
        n2 = np.arange(R2)[None, None, :, None]
        n3 = np.arange(HALF)[None, None, None, :]
        phi = 2.0 * np.pi * (((16 * n2 + HALF * hf + n3) * k2) % 256) / 256
        delta = (n3o == n3).astype(np.float64)
        scale = 1.0 / np.sqrt(R2)
        cos = (np.cos(phi) * delta * scale).reshape(R2 * HALF, R2 * HALF)
        sin = (np.sin(phi) * delta * scale).reshape(R2 * HALF, R2 * HALF)
        mats.append(_complex_block(cos, sin))
    return np.stack(mats)


def _stage3_matrix():
    k3 = np.arange(R3)[:, None, None, None]
    k1o = np.arange(HALF)[None, :, None, None]
    k1 = np.arange(HALF)[None, None, :, None]
    n3 = np.arange(R3)[None, None, None, :]
    psi = 2.0 * np.pi * ((n3 * k3) % R3) / R3
    delta = (k1o == k1).astype(np.float64)
    scale = 1.0 / np.sqrt(R3)
    cos = (np.cos(psi) * delta * scale).reshape(R3 * HALF, HALF * R3)
    sin = (np.sin(psi) * delta * scale).reshape(R3 * HALF, HALF * R3)
    return _complex_block(cos, sin)


def _channel_matrices():
    c = np.arange(FGROUP_DIM)
    ang = 2.0 * np.pi * ((c[:, None] * c[None, :]) % FGROUP_DIM) / FGROUP_DIM
    scale = 1.0 / np.sqrt(FGROUP_DIM)
    return np.cos(ang) * scale, np.sin(ang) * scale


PROJ_TM = 1024
PROJ_TN = 1024


def _rmsnorm_rows(x, g):
    ms = jnp.mean(x * x, axis=-1, keepdims=True)
    return x * lax.rsqrt(ms + EPS) * g


PROJ_CH = 256


def _proj_qk_kernel(x_ref, g_ref, w_ref, gain_ref, *refs, n_cast):
    cast_in, (o_ref, h_ref), cast_out = refs[:n_cast], refs[n_cast:n_cast + 2], refs[n_cast + 2:]
    for src, dst in zip(cast_in, cast_out):
        dst[...] = src[...].astype(bf16)
    j = pl.program_id(1)

    @pl.when(j == 0)
    def _():
        h_ref[...] = _rmsnorm_rows(x_ref[...], g_ref[...]).astype(bf16)

    gain = gain_ref[pl.ds(j, 1), :]
    lane = lax.broadcasted_iota(jnp.int32, (1, LANES), 1)
    lo = lane < HEAD_DIM
    h = h_ref[...]
    for c in range(PROJ_TN // PROJ_CH):
        acc = jnp.dot(h, w_ref[:, c * PROJ_CH:(c + 1) * PROJ_CH], preferred_element_type=f32)
        for p in range(PROJ_CH // LANES):
            a = acc[:, p * LANES:(p + 1) * LANES]
            sq = a * a
            s0 = jnp.sum(jnp.where(lo, sq, 0.0), axis=-1, keepdims=True)
            s1 = jnp.sum(jnp.where(lo, 0.0, sq), axis=-1, keepdims=True)
            r0 = lax.rsqrt(s0 * (1.0 / HEAD_DIM) + EPS)
            r1 = lax.rsqrt(s1 * (1.0 / HEAD_DIM) + EPS)
            col = c * PROJ_CH + p * LANES
            o_ref[:, col:col + LANES] = (a * jnp.where(lo, r0, r1) * gain).astype(bf16)


def _proj_act_kernel(h_ref, w_ref, *refs, sigmoid, n_cast):
    cast_in, o_ref, cast_out = refs[:n_cast], refs[n_cast], refs[n_cast + 1:]
    h = h_ref[...]
    for c in range(PROJ_TN // PROJ_CH):
        cols = slice(c * PROJ_CH, (c + 1) * PROJ_CH)
        acc = jnp.dot(h, w_ref[:, cols], preferred_element_type=f32)
        if sigmoid:
            acc = 0.5 * jnp.tanh(0.5 * acc) + 0.5
        o_ref[:, cols] = acc.astype(bf16)
    for src, dst in zip(cast_in, cast_out):
        dst[...] = src[...].astype(bf16)


def _cast_specs(cast, grid):
    nsteps = grid[0] * grid[1]
    specs = []
    for w in cast:
        rows = w.shape[0] // nsteps
        assert rows * nsteps == w.shape[0] and rows % CHUNK == 0
        specs.append(pl.BlockSpec((rows, w.shape[1]), lambda i, j: (i * grid[1] + j, 0)))
    return specs


def _proj_qk(x2d, g_mix, w_qk_bf, qk_gain, cast=()):
    m = x2d.shape[0]
    grid = (m // PROJ_TM, 2 * ATT_WIDTH // PROJ_TN)
    cast_specs = _cast_specs(cast, grid)
    return pl.pallas_call(
        functools.partial(_proj_qk_kernel, n_cast=len(cast)),
        grid=grid,
        in_specs=[
            pl.BlockSpec((PROJ_TM, D_MODEL), lambda i, j: (i, 0)),
            pl.BlockSpec((1, D_MODEL), lambda i, j: (0, 0)),
            pl.BlockSpec((D_MODEL, PROJ_TN), lambda i, j: (0, j)),
            pl.BlockSpec((2, LANES), lambda i, j: (0, 0)),
        ] + cast_specs,
        out_specs=[pl.BlockSpec((PROJ_TM, PROJ_TN), lambda i, j: (i, j)),
                   pl.BlockSpec((PROJ_TM, D_MODEL), lambda i, j: (i, 0))] + cast_specs,
        out_shape=[jax.ShapeDtypeStruct((m, 2 * ATT_WIDTH), bf16),
                   jax.ShapeDtypeStruct((m, D_MODEL), bf16)]
        + [jax.ShapeDtypeStruct(w.shape, bf16) for w in cast],
        compiler_params=_cparams(("parallel", "arbitrary")),
        name="proj_qk",
    )(x2d, g_mix, w_qk_bf, qk_gain, *cast)


def _proj_act(h, w_in_bf, col0, width, sigmoid, name, cast=()):
    m = h.shape[0]
    blk0 = col0 // PROJ_TN
    grid = (m // PROJ_TM, width // PROJ_TN)
    cast_specs = _cast_specs(cast, grid)
    return pl.pallas_call(
        functools.partial(_proj_act_kernel, sigmoid=sigmoid, n_cast=len(cast)),
        grid=grid,
        in_specs=[
            pl.BlockSpec((PROJ_TM, D_MODEL), lambda i, j: (i, 0)),
            pl.BlockSpec((D_MODEL, PROJ_TN), lambda i, j: (0, j + blk0)),
        ] + cast_specs,
        out_specs=[pl.BlockSpec((PROJ_TM, PROJ_TN), lambda i, j: (i, j))] + cast_specs,
        out_shape=[jax.ShapeDtypeStruct((m, width), bf16)]
        + [jax.ShapeDtypeStruct(w.shape, bf16) for w in cast],
        compiler_params=_cparams(("parallel", "arbitrary")),
        name=name,
    )(h, w_in_bf, *cast)


ATT_QROWS = 8
ATT_TQ = ATT_QROWS * GRID_W
ATT_WROWS = ATT_QROWS + WIN_ROWS - 1
ATT_TK = ATT_WROWS * GRID_W
ATT_NKEY = WIN_ROWS * GRID_W
ATT_RPI = 2
ATT_AHEAD = 3
ATT_SLOTS = ATT_AHEAD + 1


def _attn_window_start(b):
    return jnp.clip(ATT_QROWS * b - WIN_ROWS // 2, 0, ROWS - ATT_WROWS)


def _attn_kernel(q_ref, k_ref, v_ref, bias_ref, o_ref, s_scr):
    b = pl.program_id(1)
    ws = _attn_window_start(b)
    lane = lax.broadcasted_iota(jnp.int32, (1, LANES), 1)
    lo = lane < HEAD_DIM

    def rows_body(it, carry):
        units = []
        for sub in range(ATT_RPI):
            i = it * ATT_RPI + sub
            r = ATT_QROWS * b + i
            rstart = jnp.clip(r - WIN_ROWS // 2, 0, ROWS - WIN_ROWS)
            off = pl.multiple_of((rstart - ws) * GRID_W, GRID_W)
            ro_base = rstart - r + (WIN_ROWS - 1)
            qoff = pl.multiple_of(i * GRID_W, GRID_W)
            for p in range(N_PAIRS):
                units.append((p, slice(p * LANES, (p + 1) * LANES), off, ro_base, qoff))

        tiles = [slice(t * LANES, (t + 1) * LANES) for t in range(ATT_NKEY // LANES)]
        slot0 = lax.shift_right_arithmetic(b, 31)
        row_max = {}

        def score_unit(u):
            p, cols, off, ro_base, qoff = units[u]
            qp = q_ref[pl.ds(qoff, GRID_W), cols]
            zero = jnp.zeros_like(qp)
            lhs = jnp.concatenate([jnp.where(lo, qp, zero), jnp.where(lo, zero, qp)], axis=0)
            kw = k_ref[pl.ds(off, ATT_NKEY), cols]
            s = lax.dot_general(lhs, kw, (((1,), (1,)), ((), ())),
                                preferred_element_type=f32)
            mx = None
            for t, tile in enumerate(tiles):
                st = s[:, tile] + bias_ref[p, ro_base + 2 * t]
                s_scr[slot0 + u % ATT_SLOTS, :, tile] = st
                mx = st if mx is None else jnp.maximum(mx, st)
            row_max[u] = jnp.max(mx, axis=-1, keepdims=True)

        def finish_unit(u):
            p, cols, off, ro_base, qoff = units[u]
            acc = None
            o = None
            for t, tile in enumerate(tiles):
                e = jnp.exp(s_scr[slot0 + u % ATT_SLOTS, :, tile] - row_max[u])
                acc = e if acc is None else acc + e
                vt = v_ref[pl.ds(off + t * LANES, LANES), cols]
                ot = jnp.dot(e.astype(bf16), vt, preferred_element_type=f32)
                o = ot if o is None else o + ot
            o = o / jnp.sum(acc, axis=-1, keepdims=True)
            o_ref[pl.ds(qoff, GRID_W), cols] = jnp.where(
                lo, o[:GRID_W], o[GRID_W:]).astype(bf16)

        for u in range(len(units) + ATT_AHEAD):
            if u < len(units):
                score_unit(u)
            if u >= ATT_AHEAD:
                finish_unit(u - ATT_AHEAD)
        return carry

    lax.fori_loop(0, ATT_QROWS // ATT_RPI, rows_body, 0)


N_ROFF = 2 * WIN_ROWS - 1
BIAS_PAD = GRID_W - WIN_COLS


def _bias_table_kernel(w_ref, valid_ref, o_ref):
    lane = lax.broadcasted_iota(jnp.int32, (GRID_W, LANES), 1)
    valid = valid_ref[...] > 0.0

    def toeplitz(hh, ro, shift):
        wb = jnp.broadcast_to(w_ref[hh, ro], (GRID_W, LANES))
        return pltpu.roll(wb, shift, 1, stride=1, stride_axis=0)

    for ro in range(N_ROFF - 1):
        halves = []
        for hh in range(2):
            t0 = toeplitz(hh, ro, LANES - (GRID_W - 1))
            t1 = toeplitz(hh, ro + 1, 1)
            halves.append(jnp.where(lane < GRID_W, t0, t1))
        o_ref[0, ro] = jnp.where(valid, jnp.concatenate(halves, axis=0), NEG_BIAS)


def _attn_bias_table(rpb):
    c = np.arange(GRID_W)
    cstart = np.clip(c - WIN_COLS // 2, 0, GRID_W - WIN_COLS)
    kc = np.arange(GRID_W)
    valid = (kc[None, :] >= cstart[:, None]) & (kc[None, :] < cstart[:, None] + WIN_COLS)
    w = jnp.pad(rpb.astype(f32), ((0, 0), (0, 0), (BIAS_PAD, LANES - BIAS_PAD - rpb.shape[-1])))
    return pl.pallas_call(
        _bias_table_kernel,
        grid=(N_PAIRS,),
        in_specs=[
            pl.BlockSpec((2, N_ROFF, 1, LANES), lambda p: (p, 0, 0, 0)),
            pl.BlockSpec((2 * GRID_W, LANES), lambda p: (0, 0)),
        ],
        out_specs=pl.BlockSpec((1, N_ROFF - 1, 2 * GRID_W, LANES), lambda p: (p, 0, 0, 0)),
        out_shape=jax.ShapeDtypeStruct((N_PAIRS, N_ROFF - 1, 2 * GRID_W, LANES), f32),
        compiler_params=_cparams(("parallel",)),
        name="bias_table",
    )(w.reshape(N_HEADS, N_ROFF, 1, LANES), jnp.asarray(np.tile(valid, (2, 2)), f32))


def _attn(zqk, zvf, bias_tab, batch):
    m = zqk.shape[0]
    nblk = ROWS // ATT_QROWS

    def kv_map(col):
        def index_map(s, b):
            return ((s * ROWS + _attn_window_start(b)) * GRID_W, col)
        return index_map

    return pl.pallas_call(
        _attn_kernel,
        grid=(batch, nblk),
        in_specs=[
            pl.BlockSpec((ATT_TQ, ATT_WIDTH), lambda s, b: (s * nblk + b, 0)),
            pl.BlockSpec((pl.Element(ATT_TK), pl.Element(ATT_WIDTH)), kv_map(ATT_WIDTH)),
            pl.BlockSpec((pl.Element(ATT_TK), pl.Element(ATT_WIDTH)), kv_map(0)),
            pl.BlockSpec(bias_tab.shape, lambda s, b: (0, 0, 0, 0)),
        ],
        out_specs=pl.BlockSpec((ATT_TQ, ATT_WIDTH), lambda s, b: (s * nblk + b, 0)),
        out_shape=jax.ShapeDtypeStruct((m, ATT_WIDTH), bf16),
        compiler_params=_cparams(("parallel", "arbitrary")),
        scratch_shapes=[pltpu.VMEM((ATT_SLOTS, 2 * GRID_W, ATT_NKEY), f32)],
        name="attn",
    )(zqk, zqk, zvf, bias_tab)


def _dft_kernel(m1_ref, m2_ref, m3_ref, x_ref, o_ref, y_ref):
    for c in range(256 // CHUNK):
        x16 = x_ref[0, :, c * CHUNK:(c + 1) * CHUNK, :].astype(f32)
        for hf in range(CHUNK // HALF):
            rows = slice(hf * HALF, (hf + 1) * HALF)
            xin = x16[:, rows, :].reshape(R1 * HALF, FGROUP_DIM).astype(bf16)
            res = jnp.dot(m1_ref[2 * c + hf], xin, preferred_element_type=f32)
            y_ref[:, :, c, rows, :] = res.reshape(2, R1, HALF, FGROUP_DIM)
    for k1 in range(R1):
        for hf in range(R3 // HALF):
            rows = slice(hf * HALF, (hf + 1) * HALF)
            xin = y_ref[:, k1, :, rows, :].reshape(2 * R2 * HALF, FGROUP_DIM).astype(bf16)
            res = jnp.dot(m2_ref[hf], xin, preferred_element_type=f32)
            y_ref[:, k1, :, rows, :] = res.reshape(2, R2, HALF, FGROUP_DIM)
    for h in range(R1 // CHUNK):
        for k2 in range(R2):
            halves = []
            for hf in range(CHUNK // HALF):
                k1s = slice(h * CHUNK + hf * HALF, h * CHUNK + (hf + 1) * HALF)
                xin = y_ref[:, k1s, k2, :, :].reshape(2 * HALF * R3, FGROUP_DIM).astype(bf16)
                res = jnp.dot(m3_ref[...], xin, preferred_element_type=f32)
                halves.append(res.reshape(2, R3, HALF, FGROUP_DIM))
            o_ref[0, :, :, k2, h * CHUNK:(h + 1) * CHUNK, :] = jnp.concatenate(
                halves, axis=2).astype(bf16)


def _dft(zvf, m1, m2, m3, batch):
    zv = zvf.reshape(batch, R1, 256, ATT_WIDTH + FOURIER_WIDTH)
    f_blk0 = ATT_WIDTH // FGROUP_DIM
    const = lambda a: pl.BlockSpec(a.shape, lambda s, g: (0,) * a.ndim,
                                   pipeline_mode=pl.Buffered(1))
    return pl.pallas_call(
        _dft_kernel,
        grid=(batch, N_FGROUPS),
        in_specs=[
            const(m1), const(m2), const(m3),
            pl.BlockSpec((1, R1, 256, FGROUP_DIM), lambda s, g: (s, 0, 0, f_blk0 + g)),
        ],
        out_specs=pl.BlockSpec((1, 2, R3, R2, R1, FGROUP_DIM), lambda s, g: (s, 0, 0, 0, 0, g)),
        out_shape=jax.ShapeDtypeStruct((batch, 2, R3, R2, R1, FOURIER_WIDTH), bf16),
        scratch_shapes=[pltpu.VMEM((2, R1, R2, R3, FGROUP_DIM), f32)],
        compiler_params=_cparams(("parallel", "arbitrary")),
        name="dft",
    )(m1, m2, m3, zv)


MERGE_TM = 512


def _merge_kernel(p_ref, att_ref, sa_ref, sb_ref, x_ref, cc_ref, sc_ref,
                  wfo_ref, wao_ref, wout_ref, o_ref):
    fm = []
    for g in range(N_FGROUPS):
        cols = slice(g * FGROUP_DIM, (g + 1) * FGROUP_DIM)
        a = jnp.dot(p_ref[0, 0, :, cols], cc_ref[...], preferred_element_type=f32)
        a = a + jnp.dot(p_ref[0, 1, :, cols], sc_ref[...], preferred_element_type=f32)
        fm.append(a.astype(bf16))
    fm = jnp.concatenate(fm, axis=1)
    ya = jnp.dot(fm, wfo_ref[...], preferred_element_type=f32)
    yb = jnp.dot(att_ref[...], wao_ref[...], preferred_element_type=f32)
    merged = sa_ref[...].astype(f32) * ya + sb_ref[...].astype(f32) * yb
    o_ref[...] = x_ref[...] + jnp.dot(merged.astype(bf16), wout_ref[...],
                                      preferred_element_type=f32)


def _merge(pz, att, zg, x2d, cc, sc, wfo, wao, wout, batch):
    m = x2d.shape[0]
    per_seq = SEQ // MERGE_TM
    pv = pz.reshape(batch, 2, SEQ, FOURIER_WIDTH)
    const = lambda shape: pl.BlockSpec(shape, lambda i: (0, 0), pipeline_mode=pl.Buffered(1))
    return pl.pallas_call(
        _merge_kernel,
        grid=(m // MERGE_TM,),
        in_specs=[
            pl.BlockSpec((1, 2, MERGE_TM, FOURIER_WIDTH),
                         lambda i: (i // per_seq, 0, i % per_seq, 0)),
            pl.BlockSpec((MERGE_TM, ATT_WIDTH), lambda i: (i, 0)),
            pl.BlockSpec((MERGE_TM, D_MODEL), lambda i: (i, 0)),
            pl.BlockSpec((MERGE_TM, D_MODEL), lambda i: (i, 1)),
            pl.BlockSpec((MERGE_TM, D_MODEL), lambda i: (i, 0)),
            const(cc.shape), const(sc.shape), const(wfo.shape), const(wao.shape), const(wout.shape),
        ],
        out_specs=pl.BlockSpec((MERGE_TM, D_MODEL), lambda i: (i, 0)),
        out_shape=jax.ShapeDtypeStruct((m, D_MODEL), f32),
        compiler_params=_cparams(("parallel",)),
        name="merge",
    )(pv, att, zg, zg, x2d, cc, sc, wfo, wao, wout)


MLP_TM = 512
MLP_TF = 1024
MLP_CH = 256


def _mlp_kernel(x_ref, g_ref, w1_ref, w2_ref, o_ref, h_ref):
    j = pl.program_id(1)

    @pl.when(j == 0)
    def _():
        x = x_ref[...]
        h_ref[...] = _rmsnorm_rows(x, g_ref[...]).astype(bf16)
        o_ref[...] = x

    h = h_ref[...]
    us = []
    for c in range(MLP_TF // MLP_CH):
        u = jnp.dot(h, w1_ref[:, c * MLP_CH:(c + 1) * MLP_CH], preferred_element_type=f32)
        u = jnp.maximum(u, 0.0)
        us.append((u * u).astype(bf16))
    o_ref[...] += jnp.dot(jnp.concatenate(us, axis=1), w2_ref[...], preferred_element_type=f32)


def _mlp(x1, g_mlp, w1_bf, w2_bf):
    m = x1.shape[0]
    return pl.pallas_call(
        _mlp_kernel,
        grid=(m // MLP_TM, D_FF // MLP_TF),
        in_specs=[
            pl.BlockSpec((MLP_TM, D_MODEL), lambda i, j: (i, 0)),
            pl.BlockSpec((1, D_MODEL), lambda i, j: (0, 0)),
            pl.BlockSpec((D_MODEL, MLP_TF), lambda i, j: (0, j)),
            pl.BlockSpec((MLP_TF, D_MODEL), lambda i, j: (j, 0)),
        ],
        out_specs=pl.BlockSpec((MLP_TM, D_MODEL), lambda i, j: (i, 0)),
        out_shape=jax.ShapeDtypeStruct((m, D_MODEL), f32),
        scratch_shapes=[pltpu.VMEM((MLP_TM, D_MODEL), bf16)],
        compiler_params=_cparams(("parallel", "arbitrary")),
        name="mlp",
    )(x1, g_mlp, w1_bf, w2_bf)


def _encoder_block(x, consts, params, late_weights):
    batch, seq, _ = x.shape
    assert seq == SEQ
    m2, m3, cc, sc = consts
    g_mix, w_qk, qk_gain, bias_tab, g_mlp = params
    to_cast = late_weights[0].dtype != bf16
    x2d = x.reshape(batch * seq, D_MODEL)
    if to_cast:
        zqk, h, w_in = _proj_qk(x2d, g_mix, w_qk, qk_gain, cast=late_weights[:1])
    else:
        w_in = late_weights[0]
        zqk, h = _proj_qk(x2d, g_mix, w_in, qk_gain)
    zvf, *mix_w = _proj_act(h, w_in, 2 * ATT_WIDTH, ATT_WIDTH + FOURIER_WIDTH, False, "proj_vf",
                            cast=late_weights[1:5] if to_cast else ())
    zg, *mlp_w = _proj_act(h, w_in, 3 * ATT_WIDTH + FOURIER_WIDTH, 2 * D_MODEL, True, "proj_gate",
                           cast=late_weights[5:] if to_cast else ())
    if to_cast:
        late_weights = (w_in,) + tuple(mix_w) + tuple(mlp_w)
    _, w_fo, w_ao, w_out, m1, w1, w2 = late_weights
    att = _attn(zqk, zvf, bias_tab, batch)
    pz = _dft(zvf, m1.reshape(256 // HALF, 2 * R1 * HALF, R1 * HALF), m2, m3, batch)
    x1 = _merge(pz, att, zg, x2d, cc, sc, w_fo, w_ao, w_out, batch)
    out = _mlp(x1, g_mlp, w1, w2)
    return out.reshape(batch, seq, D_MODEL), late_weights


def kernel(x_prompt, x_sample, g_mix, w_in, q_norm, k_norm, rpb, w_fo, w_ao, w_out, g_mlp, w1, w2):
    cc_np, sc_np = _channel_matrices()
    consts = tuple(
        jnp.asarray(a, f32).astype(bf16)
        for a in (_stage2_matrices(), _stage3_matrix(), cc_np, sc_np))
    m1 = jnp.asarray(_stage1_matrices().reshape(-1, R1 * HALF), f32)
    y_prompt, y_sample = x_prompt, x_sample
    for l in range(g_mix.shape[0]):
        qk_gain = jnp.stack([
            jnp.tile(q_norm[l].astype(f32), 2) * (HEAD_DIM ** -0.5),
            jnp.tile(k_norm[l].astype(f32), 2),
        ])
        params = (
            g_mix[l].astype(f32).reshape(1, D_MODEL), w_in[l][:, :2 * ATT_WIDTH].astype(bf16),
            qk_gain, _attn_bias_table(rpb[l]), g_mlp[l].astype(f32).reshape(1, D_MODEL),
        )
        late_weights = (w_in[l].astype(f32), w_fo[l].astype(f32), w_ao[l].astype(f32),
                        w_out[l].astype(f32), m1, w1[l].astype(f32), w2[l].astype(f32))
        y_prompt, late_weights = _encoder_block(y_prompt, consts, params, late_weights)
        y_sample, late_weights = _encoder_block(y_sample, consts, params, late_weights)
    return (y_prompt, y_sample)
```
